```python
import math
import jax
import jax.numpy as jnp
from jax import lax
import numpy as np

D_MODEL = 1024
BATCH = 2
SEQ = 16384
DEPTH = 2

GRID_W = 64
CTX_LEN = 256
HEAD_DIM = 64
ROPE_THETA = 10000.0
NORM_EPS = 1e-6
NEG_INF = -1e30

A_HEADS = 4
A_VDIM = 2 * HEAD_DIM
A_WIDTH = A_HEADS * A_VDIM
Q_BLOCK = 128

B_KV_HEADS = 2
B_GROUP = 4
B_WIDTH = B_KV_HEADS * B_GROUP * HEAD_DIM
B_KV_WIDTH = B_KV_HEADS * HEAD_DIM
WINDOW = 128
WIN_BLOCK = 128

C_HEADS = 8
C_WIDTH = C_HEADS * HEAD_DIM
NA_ROWS = 8
NA_COLS = 16
NA_COL_BLOCK = 16
NA_COL_SPAN = NA_COL_BLOCK + NA_COLS

D_HEADS = 8
D_HEAD = 64
D_WIDTH = D_HEADS * D_HEAD
D_DECAY_LORA = 32
D_ICL_LORA = 32
D_GATE_LORA = 96
D_GN_EPS = 64e-5

D_FF = ((8 * D_MODEL + 3 * 256 - 1) // (3 * 256)) * 256

AB_SPLITS = (A_WIDTH, 2 * A_WIDTH, 3 * A_WIDTH, 3 * A_WIDTH + B_WIDTH, 3 * A_WIDTH + B_WIDTH + B_KV_WIDTH)
AB_IN = 3 * A_WIDTH + B_WIDTH + 2 * B_KV_WIDTH
AB_OUT = A_WIDTH + B_WIDTH
D_SPLITS = (D_WIDTH, 2 * D_WIDTH, 3 * D_WIDTH, 3 * D_WIDTH + 2 * D_DECAY_LORA, 3 * D_WIDTH + 2 * D_DECAY_LORA + 2 * D_ICL_LORA)
D_COLS = 3 * D_WIDTH + 2 * D_DECAY_LORA + 2 * D_ICL_LORA + D_GATE_LORA
CD_IN = 3 * C_WIDTH + D_COLS
CD_OUT = C_WIDTH + D_WIDTH

kernel_name = 'hybrid_diffattn_window_natten_rwkv7_prefix'


def rms_norm(x, gain):
    xf = x.astype(jnp.float32)
    y = xf * lax.rsqrt(jnp.mean(xf * xf, axis=-1, keepdims=True) + NORM_EPS)
    return (y * gain.astype(jnp.float32)).astype(x.dtype)


def modulate(h, shift, scale):
    return h * (1.0 + scale) + shift


def swiglu(h, w_gate, w_up, w_down):
    return (jax.nn.silu(h @ w_gate) * (h @ w_up)) @ w_down


def axial_rope_tables(n_tokens):
    axis_dim = HEAD_DIM // 2
    freqs = ROPE_THETA ** (-jnp.arange(0, axis_dim, 2, dtype=jnp.float32) / axis_dim)
    t = jnp.arange(n_tokens, dtype=jnp.int32)
    ang_r = (t // GRID_W).astype(jnp.float32)[:, None] * freqs
    ang_c = (t % GRID_W).astype(jnp.float32)[:, None] * freqs
    return (jnp.cos(ang_r), jnp.sin(ang_r), jnp.cos(ang_c), jnp.sin(ang_c))


def _rotate_half(x, cos, sin):
    x1, x2 = jnp.split(x, 2, axis=-1)
    return jnp.concatenate([x1 * cos - x2 * sin, x2 * cos + x1 * sin], axis=-1)


def apply_axial_rope(x, tables):
    shape = (x.shape[1],) + (1,) * (x.ndim - 3) + (tables[0].shape[-1],)
    cr, sr, cc, sc = [tb.reshape(shape) for tb in tables]
    half = x.shape[-1] // 2
    xf = x.astype(jnp.float32)
    out = jnp.concatenate([_rotate_half(xf[..., :half], cr, sr), _rotate_half(xf[..., half:], cc, sc)], axis=-1)
    return out.astype(x.dtype)


def _diff_softmax_attend(q, k, v, lam):
    s = jnp.einsum('bqhcd,bkhcd->bhcqk', q, k, preferred_element_type=jnp.float32) * (HEAD_DIM ** -0.5)
    p = jax.nn.softmax(s, axis=-1)
    attn = p[:, :, 0] - lam * p[:, :, 1]
    return jnp.einsum('bhqk,bkhe->bqhe', attn, v.astype(jnp.float32))


def _diff_post(o, subln, lambda_init):
    b, t = o.shape[:2]
    return (rms_norm(o, subln) * (1.0 - lambda_init)).reshape(b, t, A_WIDTH)


def differential_attention(q, k, v, k_ctx, v_ctx, lam, subln, lambda_init):
    b, s_len = q.shape[:2]
    nb = s_len // Q_BLOCK
    k_all = jnp.concatenate([k_ctx, k], axis=1)
    v_all = jnp.concatenate([v_ctx, v], axis=1)
    qb = jnp.moveaxis(q.reshape(b, nb, Q_BLOCK, A_HEADS, 2, HEAD_DIM), 1, 0)
    o = lax.map(lambda blk: _diff_softmax_attend(blk, k_all, v_all, lam), qb)
    o = jnp.moveaxis(o, 0, 1).reshape(b, s_len, A_HEADS, A_VDIM)
    return _diff_post(o, subln, lambda_init)


def windowed_gqa_sink(q, k, v, k_ctx, v_ctx, sink):
    b, s_len, g, j, d = q.shape
    nb = s_len // WIN_BLOCK
    scale = d ** -0.5
    sink_l = sink.astype(jnp.float32).reshape(1, g, j, 1, 1)

    def band(t):
        tp = jnp.pad(t, ((0, 0), (WIN_BLOCK, WIN_BLOCK), (0, 0), (0, 0))).reshape(b, nb + 2, WIN_BLOCK, g, d)
        tb = jnp.concatenate([tp[:, :-2], tp[:, 1:-1], tp[:, 2:]], axis=2)
        return jnp.moveaxis(tb, 1, 0)

    q_blocks = jnp.moveaxis(q.reshape(b, nb, WIN_BLOCK, g, j, d), 1, 0)
    q_offs = jnp.arange(WIN_BLOCK)
    k_offs = jnp.arange(3 * WIN_BLOCK) - WIN_BLOCK
    n_win = 3 * WIN_BLOCK
    n_ctx = k_ctx.shape[1]

    def block(args):
        qb, kb, vb, bi = args
        qpos = bi * WIN_BLOCK + q_offs
        kpos = bi * WIN_BLOCK + k_offs
        valid = (kpos >= 0)[None, :] & (kpos < s_len)[None, :] & (jnp.abs(qpos[:, None] - kpos[None, :]) <= WINDOW)
        s_win = jnp.einsum('bqgjd,bkgd->bgjqk', qb, kb, preferred_element_type=jnp.float32) * scale
        s_win = jnp.where(valid, s_win, NEG_INF)
        s_ctx = jnp.einsum('bqgjd,bkgd->bgjqk', qb, k_ctx, preferred_element_type=jnp.float32) * scale
        s_snk = jnp.broadcast_to(sink_l, s_win.shape[:-1] + (1,))
        p = jax.nn.softmax(jnp.concatenate([s_win, s_ctx, s_snk], axis=-1), axis=-1)
        o = jnp.einsum('bgjqk,bkgd->bqgjd', p[..., :n_win], vb.astype(jnp.float32))
        return o + jnp.einsum('bgjqk,bkgd->bqgjd', p[..., n_win:n_win + n_ctx], v_ctx.astype(jnp.float32))

    o = lax.map(block, (q_blocks, band(k), band(v), jnp.arange(nb)))
    return jnp.moveaxis(o, 0, 1).reshape(b, s_len, g * j * d)


def sink_ctx_attention(q, k, v, sink):
    b, t, g, j, d = q.shape
    s = jnp.einsum('bqgjd,bkgd->bgjqk', q, k, preferred_element_type=jnp.float32) * (d ** -0.5)
    snk = jnp.broadcast_to(sink.astype(jnp.float32).reshape(1, g, j, 1, 1), s.shape[:-1] + (1,))
    p = jax.nn.softmax(jnp.concatenate([s, snk], axis=-1), axis=-1)[..., :-1]
    return jnp.einsum('bgjqk,bkgd->bqgjd', p, v.astype(jnp.float32)).reshape(b, t, g * j * d)


def neighbourhood_attention(q, k, v, k_ctx, v_ctx, rpb):
    b, s_len, h, d = q.shape
    rows = s_len // GRID_W
    kr = min(NA_ROWS, rows)
    n_cb = GRID_W // NA_COL_BLOCK
    scale = d ** -0.5
    qg = q.reshape(b, rows, GRID_W, h, d)
    kg = k.reshape(b, rows, GRID_W, h, d)
    vg = v.reshape(b, rows, GRID_W, h, d)
    qcol = np.arange(GRID_W).reshape(n_cb, NA_COL_BLOCK)
    c0 = np.clip(np.arange(n_cb) * NA_COL_BLOCK - NA_COLS // 2, 0, GRID_W - NA_COL_SPAN)
    kcol = c0[:, None] + np.arange(NA_COL_SPAN)
    cstart = np.clip(qcol - NA_COLS // 2, 0, GRID_W - NA_COLS)
    in_win = (kcol[:, None, :] >= cstart[:, :, None]) & (kcol[:, None, :] < cstart[:, :, None] + NA_COLS)
    col_idx = np.clip(kcol[:, None, :] - qcol[:, :, None], 1 - NA_COLS, NA_COLS - 1) + NA_COLS - 1
    mask = np.broadcast_to(in_win[:, :, None, :], (n_cb, NA_COL_BLOCK, kr, NA_COL_SPAN)).reshape(n_cb, NA_COL_BLOCK, kr * NA_COL_SPAN)
    rpb_f = rpb.astype(jnp.float32)
    n_nb = kr * NA_COL_SPAN

    def gather_rows(t_grid, r0):
        rows_blk = lax.dynamic_slice_in_dim(t_grid, r0, kr, axis=1)
        blk = rows_blk[:, :, kcol]
        return jnp.moveaxis(blk, 2, 1).reshape(b, n_cb, n_nb, h, d)

    def row(i):
        r0 = jnp.clip(i - kr // 2, 0, rows - kr)
        qi = lax.dynamic_index_in_dim(qg, i, axis=1, keepdims=False).reshape(b, n_cb, NA_COL_BLOCK, h, d)
        ki = gather_rows(kg, r0)
        vi = gather_rows(vg, r0)
        row_idx = r0 + jnp.arange(kr) - i + NA_ROWS - 1
        bias = rpb_f[:, row_idx[None, None, :, None], col_idx[:, :, None, :]].reshape(h, n_cb, NA_COL_BLOCK, n_nb)
        s_nb = jnp.einsum('bmqhd,bmkhd->bhmqk', qi, ki, preferred_element_type=jnp.float32) * scale + bias
        s_nb = jnp.where(mask, s_nb, NEG_INF)
        s_cx = jnp.einsum('bmqhd,bkhd->bhmqk', qi, k_ctx, preferred_element_type=jnp.float32) * scale
        p = jax.nn.softmax(jnp.concatenate([s_nb, s_cx], axis=-1), axis=-1)
        o = jnp.einsum('bhmqk,bmkhd->bmqhd', p[..., :n_nb], vi.astype(jnp.float32))
        o = o + jnp.einsum('bhmqk,bkhd->bmqhd', p[..., n_nb:], v_ctx.astype(jnp.float32))
        return o.reshape(b, GRID_W, h, d)

    o = lax.map(row, jnp.arange(rows))
    return jnp.moveaxis(o, 0, 1).reshape(b, s_len, h * d)


def ctx_softmax_attention(q, k, v):
    b, t, h, d = q.shape
    s = jnp.einsum('bqhd,bkhd->bhqk', q, k, preferred_element_type=jnp.float32) * (d ** -0.5)
    return jnp.einsum('bhqk,bkhd->bqhd', jax.nn.softmax(s, axis=-1), v.astype(jnp.float32)).reshape(b, t, h * d)


def _bidir_token_shift(x, mu):
    prev = jnp.pad(x[:, :-1], ((0, 0), (1, 0), (0, 0)))
    nxt = jnp.pad(x[:, 1:], ((0, 0), (0, 1), (0, 0)))
    return x + mu * (0.5 * (prev + nxt) - x)


def _rwkv7_prepare(cols, mu, w0, w2, a0, a2, g2, k_k, k_a):
    b, t, _ = cols.shape
    xs = _bidir_token_shift(cols.astype(jnp.float32), mu.astype(jnp.float32))
    r, k, v, wd, ad, gd = jnp.split(xs, D_SPLITS, axis=-1)
    wd = wd.reshape(b, t, 2, D_DECAY_LORA)
    ad = ad.reshape(b, t, 2, D_ICL_LORA)
    w = -jax.nn.softplus(-(w0 + jnp.einsum('btzl,zlc->btzc', jnp.tanh(wd), w2))) - 0.5
    a = jax.nn.sigmoid(a0 + jnp.einsum('btzl,zlc->btzc', ad, a2))
    g = jax.nn.sigmoid(gd) @ g2
    kk = (k * k_k).reshape(b, t, D_HEADS, D_HEAD)
    kk = kk / jnp.maximum(jnp.sqrt(jnp.sum(kk * kk, axis=-1, keepdims=True)), 1e-12)
    k_dir = k[:, :, None, :] * (1.0 + (a - 1.0) * k_a)

    def heads(z):
        return jnp.moveaxis(z.reshape(b, t, 2, D_HEADS, D_HEAD), 2, 0)

    r_h = r.reshape(b, t, D_HEADS, D_HEAD)
    v_h = v.reshape(b, t, D_HEADS, D_HEAD)
    return r_h, heads(k_dir), v_h, heads(jnp.exp(-jnp.exp(w))), kk, heads(a), g


def _rwkv7_scan(r, decay, k, v, a, b, s0):
    xs = tuple(jnp.moveaxis(z, 2, 0) for z in (r, decay, k, v, a, b))

    def step(s, inp):
        r_t, w_t, k_t, v_t, a_t, b_t = inp
        sa = jnp.einsum('zbhvk,zbhk->zbhv', s, a_t)
        s = s * w_t[..., None, :] + sa[..., :, None] * b_t[..., None, :] + v_t[..., :, None] * k_t[..., None, :]
        return s, jnp.einsum('zbhvk,zbhk->zbhv', s, r_t)

    s_fin, ys = lax.scan(step, s0, xs)
    return s_fin, jnp.moveaxis(ys, 0, 2)


def _orient(z):
    return jnp.concatenate([z[:1], jnp.flip(z[1:], axis=2)], axis=0)


def _rwkv7_bidir_scan(r_h, k_dir, v_h, decay, kk, a_h, s0):
    inputs = (jnp.stack([r_h, r_h]), decay, k_dir, jnp.stack([v_h, v_h]), jnp.stack([-kk, -kk]), kk[None] * a_h)
    s_fin, y = _rwkv7_scan(*[_orient(z) for z in inputs], s0)
    return s_fin, _orient(y)


def _rwkv7_output(y, r_h, k_dir, v_h, g, r_k, ln_w, ln_b):
    b, t = r_h.shape[:2]
    y = jnp.sum(y, axis=0)
    mean = jnp.mean(y, axis=-1, keepdims=True)
    var = jnp.mean(jnp.square(y - mean), axis=-1, keepdims=True)
    yn = ((y - mean) * lax.rsqrt(var + D_GN_EPS)).reshape(b, t, D_WIDTH) * ln_w + ln_b
    bonus = jnp.sum(jnp.sum(r_h[None] * k_dir * r_k, axis=-1, keepdims=True) * v_h[None], axis=0)
    return (yn + bonus.reshape(b, t, D_WIDTH)) * g


def ab_mixer(h_lat, h_ctx, w_in, w_out, a_qn, a_kn, a_lam, a_subln, b_qn, b_kn, b_sink, rope, lambda_init, need_ctx):
    def split(p):
        b, t, _ = p.shape
        aq, ak, av, bq, bk, bv = jnp.split(p, AB_SPLITS, axis=-1)
        aq = rms_norm(aq.reshape(b, t, A_HEADS, 2, HEAD_DIM), a_qn)
        ak = rms_norm(ak.reshape(b, t, A_HEADS, 2, HEAD_DIM), a_kn)
        av = av.reshape(b, t, A_HEADS, A_VDIM)
        bq = rms_norm(bq.reshape(b, t, B_KV_HEADS, B_GROUP, HEAD_DIM), b_qn)
        bk = rms_norm(bk.reshape(b, t, B_KV_HEADS, HEAD_DIM), b_kn)
        bv = bv.reshape(b, t, B_KV_HEADS, HEAD_DIM)
        return aq, ak, av, bq, bk, bv

    aq, ak, av, bq, bk, bv = split(h_lat @ w_in)
    aq, ak, bq, bk = [apply_axial_rope(z, rope) for z in (aq, ak, bq, bk)]
    aq_c, ak_c, av_c, bq_c, bk_c, bv_c = split(h_ctx @ w_in)
    lam_f = a_lam.astype(jnp.float32)
    lam = jnp.exp(jnp.sum(lam_f[0] * lam_f[1])) - jnp.exp(jnp.sum(lam_f[2] * lam_f[3])) + lambda_init
    a_lat = differential_attention(aq, ak, av, ak_c, av_c, lam, a_subln, lambda_init)
    b_lat = windowed_gqa_sink(bq, bk, bv, bk_c, bv_c, b_sink)
    o_lat = jnp.concatenate([a_lat, b_lat], axis=-1).astype(h_lat.dtype) @ w_out
    if not need_ctx:
        return o_lat, None
    a_ctx = _diff_post(_diff_softmax_attend(aq_c, ak_c, av_c, lam), a_subln, lambda_init)
    b_ctx = sink_ctx_attention(bq_c, bk_c, bv_c, b_sink)
    o_ctx = jnp.concatenate([a_ctx, b_ctx], axis=-1).astype(h_ctx.dtype) @ w_out
    return o_lat, o_ctx


def cd_mixer(h_lat, h_ctx, w_in, w_out, c_qn, c_kn, c_rpb, d_mu, d_w0, d_w2, d_a0, d_a2, d_g2, d_k_k, d_k_a, d_r_k, d_ln_w, d_ln_b, need_ctx):
    def split(p):
        b, t, _ = p.shape
        cq, ck, cv, dcols = jnp.split(p, (C_WIDTH, 2 * C_WIDTH, 3 * C_WIDTH), axis=-1)
        cq = rms_norm(cq.reshape(b, t, C_HEADS, HEAD_DIM), c_qn)
        ck = rms_norm(ck.reshape(b, t, C_HEADS, HEAD_DIM), c_kn)
        return cq, ck, cv.reshape(b, t, C_HEADS, HEAD_DIM), dcols

    cq, ck, cv, d_lat_cols = split(h_lat @ w_in)
    cq_c, ck_c, cv_c, d_ctx_cols = split(h_ctx @ w_in)
    c_lat = neighbourhood_attention(cq, ck, cv, ck_c, cv_c, c_rpb)

    prep_c = _rwkv7_prepare(d_ctx_cols, d_mu, d_w0, d_w2, d_a0, d_a2, d_g2, d_k_k, d_k_a)
    prep_l = _rwkv7_prepare(d_lat_cols, d_mu, d_w0, d_w2, d_a0, d_a2, d_g2, d_k_k, d_k_a)
    s0 = jnp.zeros((2, h_lat.shape[0], D_HEADS, D_HEAD, D_HEAD), jnp.float32)
    s_ctx, y_ctx = _rwkv7_bidir_scan(*prep_c[:6], s0)
    _, y_lat = _rwkv7_bidir_scan(*prep_l[:6], s_ctx)
    d_lat = _rwkv7_output(y_lat, prep_l[0], prep_l[1], prep_l[2], prep_l[6], d_r_k, d_ln_w, d_ln_b)
    o_lat = jnp.concatenate([c_lat, d_lat], axis=-1).astype(h_lat.dtype) @ w_out
    if not need_ctx:
        return o_lat, None
    c_ctx_o = ctx_softmax_attention(cq_c, ck_c, cv_c)
    d_ctx_o = _rwkv7_output(y_ctx, prep_c[0], prep_c[1], prep_c[2], prep_c[6], d_r_k, d_ln_w, d_ln_b)
    o_ctx = jnp.concatenate([c_ctx_o, d_ctx_o], axis=-1).astype(h_ctx.dtype) @ w_out
    return o_lat, o_ctx


def setup_inputs(seed: int = 0) -> dict:
    key = jax.random.key(seed)
    ks = list(jax.random.split(key, 40))
    n_ab = (DEPTH + 1) // 2
    n_cd = DEPTH // 2

    def nrm(idx, shape, scale):
        return jax.random.normal(ks[idx], shape, jnp.float32) * scale

    return {
        'x': nrm(0, (BATCH, SEQ, D_MODEL), 1.0),
        'c': nrm(1, (BATCH, D_MODEL), 1.0),
        'ctx': nrm(2, (BATCH, CTX_LEN, D_MODEL), 1.0),
        'c_ctx': nrm(3, (D_MODEL,), 1.0),
        'ada_w': nrm(4, (DEPTH, D_MODEL, 6 * D_MODEL), 0.5 * D_MODEL ** -0.5),
        'ada_b': nrm(5, (DEPTH, 6 * D_MODEL), 0.02),
        'norm_mix': 1.0 + nrm(6, (DEPTH, D_MODEL), 0.02),
        'norm_ffn': 1.0 + nrm(7, (DEPTH, D_MODEL), 0.02),
        'ffn_w_gate': nrm(8, (DEPTH, D_MODEL, D_FF), D_MODEL ** -0.5),
        'ffn_w_up': nrm(9, (DEPTH, D_MODEL, D_FF), D_MODEL ** -0.5),
        'ffn_w_down': nrm(10, (DEPTH, D_FF, D_MODEL), D_FF ** -0.5),
        'ab_w_in': nrm(11, (n_ab, D_MODEL, AB_IN), D_MODEL ** -0.5),
        'ab_w_out': nrm(12, (n_ab, AB_OUT, D_MODEL), AB_OUT ** -0.5),
        'a_q_norm': 1.0 + nrm(13, (n_ab, HEAD_DIM), 0.02),
        'a_k_norm': 1.0 + nrm(14, (n_ab, HEAD_DIM), 0.02),
        'a_lambda': nrm(15, (n_ab, 4, HEAD_DIM), 0.1),
        'a_subln': 1.0 + nrm(16, (n_ab, A_VDIM), 0.02),
        'b_q_norm': 1.0 + nrm(17, (n_ab, HEAD_DIM), 0.02),
        'b_k_norm': 1.0 + nrm(18, (n_ab, HEAD_DIM), 0.02),
        'b_sink': nrm(19, (n_ab, B_KV_HEADS * B_GROUP), 1.0),
        'cd_w_in': nrm(20, (n_cd, D_MODEL, CD_IN), D_MODEL ** -0.5),
        'cd_w_out': nrm(21, (n_cd, CD_OUT, D_MODEL), CD_OUT ** -0.5),
        'c_q_norm': 1.0 + nrm(22, (n_cd, HEAD_DIM), 0.02),
        'c_k_norm': 1.0 + nrm(23, (n_cd, HEAD_DIM), 0.02),
        'c_rpb': nrm(24, (n_cd, C_HEADS, 2 * NA_ROWS - 1, 2 * NA_COLS - 1), 0.5),
        'd_mu': jax.random.uniform(ks[25], (n_cd, D_COLS), jnp.float32, 0.0, 1.0),
        'd_w0': jax.random.uniform(ks[26], (n_cd, 2, D_WIDTH), jnp.float32, -6.0, 1.0),
        'd_w2': nrm(27, (n_cd, 2, D_DECAY_LORA, D_WIDTH), 0.1),
        'd_a0': nrm(28, (n_cd, 2, D_WIDTH), 0.5),
        'd_a2': nrm(29, (n_cd, 2, D_ICL_LORA, D_WIDTH), 0.1),
        'd_g2': nrm(30, (n_cd, D_GATE_LORA, D_WIDTH), D_GATE_LORA ** -0.5),
        'd_k_k': 0.85 + nrm(31, (n_cd, D_WIDTH), 0.02),
        'd_k_a': 1.0 + nrm(32, (n_cd, D_WIDTH), 0.02),
        'd_r_k': nrm(33, (n_cd, D_HEADS, D_HEAD), 0.1),
        'd_ln_w': 1.0 + nrm(34, (n_cd, D_WIDTH), 0.02),
        'd_ln_b': nrm(35, (n_cd, D_WIDTH), 0.02),
    }


def reference(x, c, ctx, c_ctx, ada_w, ada_b, norm_mix, norm_ffn, ffn_w_gate, ffn_w_up, ffn_w_down,
              ab_w_in, ab_w_out, a_q_norm, a_k_norm, a_lambda, a_subln, b_q_norm, b_k_norm, b_sink,
              cd_w_in, cd_w_out, c_q_norm, c_k_norm, c_rpb, d_mu, d_w0, d_w2, d_a0, d_a2, d_g2,
              d_k_k, d_k_a, d_r_k, d_ln_w, d_ln_b):
    rope = axial_rope_tables(x.shape[1])
    cond_lat = jax.nn.silu(c)[:, None, :]
    cond_ctx = jax.nn.silu(c_ctx)
    x_lat, x_ctx = x, ctx
    for layer in range(DEPTH):
        need_ctx = layer < DEPTH - 1
        i = layer // 2
        sh_l, sc_l, g_l, fsh_l, fsc_l, fg_l = jnp.split(cond_lat @ ada_w[layer] + ada_b[layer], 6, axis=-1)
        sh_c, sc_c, g_c, fsh_c, fsc_c, fg_c = jnp.split(cond_ctx @ ada_w[layer] + ada_b[layer], 6, axis=-1)
        h_lat = modulate(rms_norm(x_lat, norm_mix[layer]), sh_l, sc_l)
        h_ctx = modulate(rms_norm(x_ctx, norm_mix[layer]), sh_c, sc_c)
        if layer % 2 == 0:
            lambda_init = 0.8 - 0.6 * math.exp(-0.3 * layer)
            o_lat, o_ctx = ab_mixer(h_lat, h_ctx, ab_w_in[i], ab_w_out[i], a_q_norm[i], a_k_norm[i], a_lambda[i],
                                    a_subln[i], b_q_norm[i], b_k_norm[i], b_sink[i], rope, lambda_init, need_ctx)
        else:
            o_lat, o_ctx = cd_mixer(h_lat, h_ctx, cd_w_in[i], cd_w_out[i], c_q_norm[i], c_k_norm[i], c_rpb[i],
                                    d_mu[i], d_w0[i], d_w2[i], d_a0[i], d_a2[i], d_g2[i], d_k_k[i], d_k_a[i],
                                    d_r_k[i], d_ln_w[i], d_ln_b[i], need_ctx)
        x_lat = x_lat + g_l * o_lat
        x_lat = x_lat + fg_l * swiglu(modulate(rms_norm(x_lat, norm_ffn[layer]), fsh_l, fsc_l),
                                      ffn_w_gate[layer], ffn_w_up[layer], ffn_w_down[layer])
        if need_ctx:
            x_ctx = x_ctx + g_c * o_ctx
            x_ctx = x_ctx + fg_c * swiglu(modulate(rms_norm(x_ctx, norm_ffn[layer]), fsh_c, fsc_c),
                                          ffn_w_gate[layer], ffn_w_up[layer], ffn_w_down[layer])
    return x_lat
```

```python
import functools
import math

import jax
import jax.numpy as jnp
import numpy as np
from jax import lax
from jax.experimental import pallas as pl
from jax.experimental.pallas import tpu as pltpu

F32 = jnp.float32
BF16 = jnp.bfloat16

D_MODEL = 1024
GRID_W = 64
HEAD_DIM = 64
ROPE_THETA = 10000.0
NORM_EPS = 1e-6
NEG_INF = -1e30
LANES = 128
A_HEADS = 4
WINDOW = 128
NA_ROWS = 8
NA_COLS = 16
D_HEADS = 8
D_WIDTH = 512
D_GN_EPS = 64e-5
VMEM_LIMIT = 56 * 1024 * 1024


def _cparams(sem):
    return pltpu.CompilerParams(dimension_semantics=sem, vmem_limit_bytes=VMEM_LIMIT)


def _dot(a, b):
    return jnp.dot(a, b, preferred_element_type=F32)


def _dot_nt(a, b):
    return lax.dot_general(a, b, (((1,), (1,)), ((), ())), preferred_element_type=F32)


def _split(x):
    hi = x.astype(BF16)
    lo = (x - hi.astype(F32)).astype(BF16)
    return hi, lo


def _dot3(a, b):
    ah, al = _split(a)
    bh, bl = _split(b)
    return _dot(ah, bh) + _dot(al, bh) + _dot(ah, bl)


def _dot3_nt(a, b):
    ah, al = _split(a)
    bh, bl = _split(b)
    return _dot_nt(ah, bh) + _dot_nt(al, bh) + _dot_nt(ah, bl)


def _dot2_exact_rhs(a, b_bf16):
    ah, al = _split(a)
    return _dot(ah, b_bf16) + _dot(al, b_bf16)


def _norm_mod(x, gain, shift, scale):
    ms = jnp.mean(x * x, axis=-1, keepdims=True)
    y = x * lax.rsqrt(ms + NORM_EPS) * gain
    return y * (1.0 + scale) + shift


def _adaln_kernel(c_ref, w_ref, b_ref, o_ref):
    c = c_ref[...]
    o_ref[0] = _dot3(jax.nn.silu(c), w_ref[0]) + b_ref[0]


def _adaln(cond, ada_w, ada_b):
    depth, d, n = ada_w.shape
    tn = 1536
    return pl.pallas_call(
        _adaln_kernel,
        grid=(depth, n // tn),
        in_specs=[
            pl.BlockSpec((8, d), lambda l, j: (0, 0)),
            pl.BlockSpec((1, d, tn), lambda l, j: (l, 0, j)),
            pl.BlockSpec((1, 1, tn), lambda l, j: (l, 0, j)),
        ],
        out_specs=pl.BlockSpec((1, 8, tn), lambda l, j: (l, 0, j)),
        out_shape=jax.ShapeDtypeStruct((depth, 8, n), F32),
        compiler_params=_cparams(("arbitrary", "arbitrary")),
        name="adaln",
    )(cond, ada_w, ada_b.reshape(depth, 1, n))


def _head_norm_rope(y, bd, hg, cos, sin):
    ss = _dot2_exact_rhs(y * y, bd)
    y = y * lax.rsqrt(ss * (1.0 / HEAD_DIM) + NORM_EPS) * hg
    lane = lax.broadcasted_iota(jnp.int32, y.shape, 1)
    first_half = (lane % 32) < 16
    partner = jnp.where(first_half, pltpu.roll(y, LANES - 16, axis=1), pltpu.roll(y, 16, axis=1))
    return y * cos + partner * sin


def _inproj_kernel(x_ref, sh_ref, sc_ref, gain_ref, w_ref, cos_ref, sin_ref, hg_ref, bd_ref, *out_refs, plan):
    h = _norm_mod(x_ref[0], gain_ref[...], sh_ref[0], sc_ref[0]).astype(BF16)
    bd = bd_ref[...]
    cos = cos_ref[...]
    sin = sin_ref[...]
    n_blocks = len(plan)
    for j0 in range(0, n_blocks, 2):
        width = min(2, n_blocks - j0) * LANES
        y2 = _dot(h, w_ref[:, j0 * LANES:j0 * LANES + width])
        for jj in range(width // LANES):
            kind, hg_idx, out_idx, out_blk = plan[j0 + jj]
            y = y2[:, jj * LANES:(jj + 1) * LANES]
            o_ref = out_refs[out_idx]
            if kind == "qk":
                y = _head_norm_rope(y, bd, hg_ref[hg_idx:hg_idx + 1, :], cos, sin)
                o_ref[0, :, out_blk * LANES:(out_blk + 1) * LANES] = y.astype(o_ref.dtype)
            elif kind == "vT":
                o_ref[0, out_blk] = y.T.astype(o_ref.dtype)
            else:
                o_ref[0, :, out_blk * LANES:(out_blk + 1) * LANES] = y.astype(o_ref.dtype)


def _inproj(x, shift, scale, gain, w, cos, sin, hg, plan, out_defs, tm):
    b, t, d = x.shape
    n = w.shape[1]
    bd = jnp.asarray(np.kron(np.eye(2), np.ones((HEAD_DIM, HEAD_DIM))), BF16)
    out_shapes, out_specs = [], []
    for kind, nblk, dt in out_defs:
        if kind == "rows":
            out_shapes.append(jax.ShapeDtypeStruct((b, t, nblk * LANES), dt))
            out_specs.append(pl.BlockSpec((1, tm, nblk * LANES), lambda bi, i: (bi, i, 0)))
        else:
            out_shapes.append(jax.ShapeDtypeStruct((b, nblk, LANES, t), dt))
            out_specs.append(pl.BlockSpec((1, nblk, LANES, tm), lambda bi, i: (bi, 0, 0, i)))
    return pl.pallas_call(
        functools.partial(_inproj_kernel, plan=tuple(plan)),
        grid=(b, t // tm),
        in_specs=[
            pl.BlockSpec((1, tm, d), lambda bi, i: (bi, i, 0)),
            pl.BlockSpec((1, 1, d), lambda bi, i: (bi, 0, 0)),
            pl.BlockSpec((1, 1, d), lambda bi, i: (bi, 0, 0)),
            pl.BlockSpec((1, d), lambda bi, i: (0, 0)),
            pl.BlockSpec((d, n), lambda bi, i: (0, 0)),
            pl.BlockSpec((tm, LANES), lambda bi, i: (i, 0)),
            pl.BlockSpec((tm, LANES), lambda bi, i: (i, 0)),
            pl.BlockSpec(hg.shape, lambda bi, i: (0, 0)),
            pl.BlockSpec((LANES, LANES), lambda bi, i: (0, 0)),
        ],
        out_specs=out_specs,
        out_shape=out_shapes,
        compiler_params=_cparams(("parallel", "arbitrary")),
        name="inproj",
    )(x, shift, scale, gain.reshape(1, d), w, cos, sin, hg, bd)


def _rope_tables(n_tokens):
    axis_dim = HEAD_DIM // 2
    freqs = ROPE_THETA ** (-jnp.arange(0, axis_dim, 2, dtype=F32) / axis_dim)
    t = jnp.arange(n_tokens, dtype=jnp.int32)
    lane = np.arange(LANES)
    d = lane % HEAD_DIM
    use_col = (d // 32) == 1
    f_idx = (d % 32) % 16
    sign = np.where((d % 32) < 16, -1.0, 1.0).astype(np.float32)
    pos = jnp.where(use_col[None, :], (t % GRID_W)[:, None], (t // GRID_W)[:, None]).astype(F32)
    ang = pos * freqs[f_idx][None, :]
    return jnp.cos(ang), jnp.sin(ang) * sign[None, :]


def _diffattn_kernel(lam_ref, q_ref, kc_ref, vc_ref, subln_ref, *rest, tk, n_lat_tiles, post_scale):
    if n_lat_tiles:
        kl_ref, vl_ref, o_ref = rest
    else:
        (o_ref,) = rest
    q = q_ref[0]
    tq = q.shape[0]
    lane = lax.broadcasted_iota(jnp.int32, q.shape, 1)
    zero = jnp.zeros_like(q)
    qq = jnp.concatenate([jnp.where(lane < HEAD_DIM, q, zero), jnp.where(lane >= HEAD_DIM, q, zero)], axis=0)

    def step(k_t, vT_t, carry):
        m, l, acc = carry
        s = _dot_nt(k_t, qq)
        m_new = jnp.maximum(m, jnp.max(s, axis=0, keepdims=True))
        alpha = jnp.exp(m - m_new)
        p = jnp.exp(s - m_new)
        l = alpha * l + jnp.sum(p, axis=0, keepdims=True)
        acc = alpha * acc + _dot(vT_t, p.astype(BF16))
        return m_new, l, acc

    carry = (jnp.full((1, 2 * tq), -jnp.inf, F32), jnp.zeros((1, 2 * tq), F32), jnp.zeros((LANES, 2 * tq), F32))
    carry = step(kc_ref[0], vc_ref[0, 0], carry)
    if n_lat_tiles:
        def body(i, c):
            off = pl.multiple_of(i * tk, tk)
            return step(kl_ref[0, pl.ds(off, tk), :], vl_ref[0, 0, :, pl.ds(off, tk)], c)
        carry = lax.fori_loop(0, n_lat_tiles, body, carry)
    _, l, acc = carry
    o = acc / l
    oT = o[:, :tq] - lam_ref[0, 0] * o[:, tq:]
    ms = jnp.mean(oT * oT, axis=0, keepdims=True)
    oT = oT * lax.rsqrt(ms + NORM_EPS) * subln_ref[...] * post_scale
    o_ref[0] = oT.T.astype(o_ref.dtype)


def _diffattn(lam, q_arr, q_blk0, kc_arr, kc_blk0, vcT, subln, post_scale, lat=None, tq=128, tk=256):
    b, t_q, _ = q_arr.shape
    n_ctx = kc_arr.shape[1]
    in_specs = [
        pl.BlockSpec(memory_space=pltpu.SMEM),
        pl.BlockSpec((1, tq, LANES), lambda bi, h, i: (bi, i, q_blk0 + h)),
        pl.BlockSpec((1, n_ctx, LANES), lambda bi, h, i: (bi, 0, kc_blk0 + h)),
        pl.BlockSpec((1, 1, LANES, n_ctx), lambda bi, h, i: (bi, h, 0, 0)),
        pl.BlockSpec((LANES, 1), lambda bi, h, i: (0, 0)),
    ]
    args = [lam, q_arr, kc_arr, vcT, subln.reshape(LANES, 1)]
    n_lat_tiles = 0
    if lat is not None:
        kl_arr, kl_blk0, vlT = lat
        s_len = kl_arr.shape[1]
        n_lat_tiles = s_len // tk
        in_specs += [
            pl.BlockSpec((1, s_len, LANES), lambda bi, h, i: (bi, 0, kl_blk0 + h)),
            pl.BlockSpec((1, 1, LANES, s_len), lambda bi, h, i: (bi, h, 0, 0)),
        ]
        args += [kl_arr, vlT]
    return pl.pallas_call(
        functools.partial(_diffattn_kernel, tk=tk, n_lat_tiles=n_lat_tiles, post_scale=post_scale),
        grid=(b, A_HEADS, t_q // tq),
        in_specs=in_specs,
        out_specs=pl.BlockSpec((1, tq, LANES), lambda bi, h, i: (bi, i, h)),
        out_shape=jax.ShapeDtypeStruct((b, t_q, A_HEADS * LANES), BF16),
        compiler_params=_cparams(("parallel", "arbitrary", "arbitrary")),
        name="diffattn",
    )(*args)


def _window_kernel(sink_ref, q_ref, kc_ref, vc_ref, *rest, with_win, n_blocks):
    if with_win:
        kp_ref, k0_ref, kn_ref, vp_ref, v0_ref, vn_ref, o_ref = rest
    else:
        (o_ref,) = rest
    i = pl.program_id(1)
    tq = q_ref.shape[1]
    n_ctx = kc_ref.shape[1]
    lane = lax.broadcasted_iota(jnp.int32, (1, LANES), 1)
    lo = lane < HEAD_DIM
    if with_win:
        r = lax.broadcasted_iota(jnp.int32, (tq, tq), 0)
        c = lax.broadcasted_iota(jnp.int32, (tq, tq), 1)
        ok_prev = jnp.logical_and(c >= r, i > 0)
        ok_next = jnp.logical_and(c <= r, i < n_blocks - 1)
        valid = jnp.concatenate([ok_prev, jnp.ones((tq, tq), jnp.bool_), ok_next,
                                 jnp.ones((tq, n_ctx), jnp.bool_)], axis=1)
    for g in range(2):
        gs = slice(g * LANES, (g + 1) * LANES)
        if with_win:
            kcat = jnp.concatenate([kp_ref[0, :, gs], k0_ref[0, :, gs], kn_ref[0, :, gs], kc_ref[0, :, gs]], axis=0)
            vcat = jnp.concatenate([vp_ref[0, :, gs], v0_ref[0, :, gs], vn_ref[0, :, gs], vc_ref[0, :, gs]], axis=0)
        else:
            kcat = kc_ref[0, :, gs]
            vcat = vc_ref[0, :, gs]
        zv = jnp.zeros_like(vcat)
        v_lo = jnp.where(lo, vcat, zv)
        v_hi = jnp.where(lo, zv, vcat)
        for cb in range(2):
            blk = g * 2 + cb
            qb = q_ref[0, :, blk * LANES:(blk + 1) * LANES]
            zq = jnp.zeros_like(qb)
            out = None
            for half, (qh, vh) in enumerate(((jnp.where(lo, qb, zq), v_lo), (jnp.where(lo, zq, qb), v_hi))):
                s = _dot_nt(qh, kcat)
                if with_win:
                    s = jnp.where(valid, s, NEG_INF)
                snk = sink_ref[blk * 2 + half:blk * 2 + half + 1, 0:1]
                m = jnp.maximum(jnp.max(s, axis=-1, keepdims=True), snk)
                e = jnp.exp(s - m)
                denom = jnp.sum(e, axis=-1, keepdims=True) + jnp.exp(snk - m)
                p = (e / denom).astype(BF16)
                contrib = _dot(p, vh)
                out = contrib if out is None else out + contrib
            o_ref[0, :, blk * LANES:(blk + 1) * LANES] = out.astype(o_ref.dtype)


def _window_attn(sink, q_arr, q_blk0, k_blk0, v_blk0, ctx_arr, with_win):
    b, t_q, _ = q_arr.shape
    n_ctx = ctx_arr.shape[1]
    tq = WINDOW
    nb = t_q // tq
    sink_tab = jnp.broadcast_to(sink.astype(F32).reshape(8, 1), (8, LANES))
    in_specs = [
        pl.BlockSpec((8, LANES), lambda bi, i: (0, 0)),
        pl.BlockSpec((1, tq, 4 * LANES), lambda bi, i: (bi, i, q_blk0 // 4)),
        pl.BlockSpec((1, n_ctx, 2 * LANES), lambda bi, i: (bi, 0, k_blk0 // 2)),
        pl.BlockSpec((1, n_ctx, 2 * LANES), lambda bi, i: (bi, 0, v_blk0 // 2)),
    ]
    args = [sink_tab, q_arr, ctx_arr, ctx_arr]
    if with_win:
        for blk0 in (k_blk0, v_blk0):
            in_specs += [
                pl.BlockSpec((1, tq, 2 * LANES), lambda bi, i, c=blk0 // 2: (bi, jnp.maximum(i - 1, 0), c)),
                pl.BlockSpec((1, tq, 2 * LANES), lambda bi, i, c=blk0 // 2: (bi, i, c)),
                pl.BlockSpec((1, tq, 2 * LANES), lambda bi, i, c=blk0 // 2: (bi, jnp.minimum(i + 1, nb - 1), c)),
            ]
            args += [q_arr, q_arr, q_arr]
    return pl.pallas_call(
        functools.partial(_window_kernel, with_win=with_win, n_blocks=nb),
        grid=(b, nb),
        in_specs=in_specs,
        out_specs=pl.BlockSpec((1, tq, 4 * LANES), lambda bi, i: (bi, i, 0)),
        out_shape=jax.ShapeDtypeStruct((b, t_q, 4 * LANES), BF16),
        compiler_params=_cparams(("parallel", "arbitrary")),
        name="window_attn",
    )(*args)


def _out_ffn_kernel(x_ref, a_ref, b_ref, woa_ref, wob_ref, gate_ref, ng_ref, fsh_ref, fsc_ref, fg_ref,
                    wg_ref, wu_ref, wd_ref, o_ref):
    o = _dot(a_ref[0], woa_ref[...]) + _dot(b_ref[0], wob_ref[...])
    x1 = x_ref[0] + gate_ref[0] * o
    h = _norm_mod(x1, ng_ref[...], fsh_ref[0], fsc_ref[0]).astype(BF16)
    act = (jax.nn.silu(_dot(h, wg_ref[...])) * _dot(h, wu_ref[...])).astype(BF16)
    o_ref[0] = x1 + fg_ref[0] * _dot(act, wd_ref[...])


def _out_ffn(x, mix_a, mix_b, wo, gate, ng, fsh, fsc, fg, wg, wu, wd, tm):
    b, t, d = x.shape
    na, nb_ = mix_a.shape[-1], mix_b.shape[-1]
    dff = wg.shape[1]
    row = lambda bi, i: (bi, i, 0)
    vec = lambda bi, i: (bi, 0, 0)
    const = lambda bi, i: (0, 0)
    resident = functools.partial(pl.BlockSpec, index_map=const, pipeline_mode=pl.Buffered(1))
    return pl.pallas_call(
        _out_ffn_kernel,
        grid=(b, t // tm),
        in_specs=[
            pl.BlockSpec((1, tm, d), row),
            pl.BlockSpec((1, tm, na), row),
            pl.BlockSpec((1, tm, nb_), row),
            resident((na, d)),
            resident((nb_, d)),
            pl.BlockSpec((1, 1, d), vec),
            resident((1, d)),
            pl.BlockSpec((1, 1, d), vec),
            pl.BlockSpec((1, 1, d), vec),
            pl.BlockSpec((1, 1, d), vec),
            resident((d, dff)),
            resident((d, dff)),
            resident((dff, d)),
        ],
        out_specs=pl.BlockSpec((1, tm, d), row),
        out_shape=jax.ShapeDtypeStruct((b, t, d), F32),
        compiler_params=_cparams(("parallel", "arbitrary")),
        name="out_ffn",
    )(x, mix_a, mix_b, wo[:na], wo[na:], gate, ng.reshape(1, d), fsh, fsc, fg, wg, wu, wd)


NA_ROWS_PER_STEP = 8


def _natten_kernel(q_ref, k_ref, v_ref, kc_ref, vc_ref, bias_ref, o_ref, *, n_rows):
    step = pl.program_id(2)
    n_win = NA_ROWS * GRID_W
    lane = lax.broadcasted_iota(jnp.int32, (1, LANES), 1)
    lo = lane < HEAD_DIM
    kc = kc_ref[0]
    vc = vc_ref[0]

    def row(rr, carry):
        i = step * NA_ROWS_PER_STEP + rr
        r0 = jnp.clip(i - NA_ROWS // 2, 0, n_rows - NA_ROWS)
        koff = pl.multiple_of(r0 * GRID_W, GRID_W)
        qoff = pl.multiple_of(rr * GRID_W, GRID_W)
        kcat = jnp.concatenate([k_ref[0, pl.ds(koff, n_win), :], kc], axis=0)
        vcat = jnp.concatenate([v_ref[0, pl.ds(koff, n_win), :], vc], axis=0)
        q = q_ref[0, pl.ds(qoff, GRID_W), :]
        zq = jnp.zeros_like(q)
        zv = jnp.zeros_like(vcat)
        out = None
        for half in range(2):
            keep = lo if half == 0 else jnp.logical_not(lo)
            s = _dot_nt(jnp.where(keep, q, zq), kcat)
            bias = bias_ref[i - r0, half]
            s = jnp.concatenate([s[:, :n_win] + bias, s[:, n_win:]], axis=1)
            m = jnp.max(s, axis=-1, keepdims=True)
            e = jnp.exp(s - m)
            p = (e / jnp.sum(e, axis=-1, keepdims=True)).astype(BF16)
            contrib = _dot(p, jnp.where(keep, vcat, zv))
            out = contrib if out is None else out + contrib
        o_ref[0, pl.ds(qoff, GRID_W), :] = out.astype(o_ref.dtype)
        return carry

    lax.fori_loop(0, NA_ROWS_PER_STEP, row, 0)


def _natten_bias(rpb):
    d_i = np.arange(NA_ROWS)[:, None, None, None]
    qj = np.arange(GRID_W)[None, :, None, None]
    kr = np.arange(NA_ROWS)[None, None, :, None]
    kj = np.arange(GRID_W)[None, None, None, :]
    row_idx = np.broadcast_to(kr - d_i + NA_ROWS - 1, (NA_ROWS, GRID_W, NA_ROWS, GRID_W))
    col_idx = np.broadcast_to(np.clip(kj - qj, 1 - NA_COLS, NA_COLS - 1) + NA_COLS - 1, row_idx.shape)
    cstart = np.clip(qj - NA_COLS // 2, 0, GRID_W - NA_COLS)
    in_win = np.broadcast_to((kj >= cstart) & (kj < cstart + NA_COLS), row_idx.shape)
    bias = rpb.astype(F32)[:, row_idx, col_idx]
    bias = jnp.where(in_win[None], bias, NEG_INF)
    return jnp.moveaxis(bias, 0, 1).reshape(NA_ROWS, rpb.shape[0], GRID_W, NA_ROWS * GRID_W)


def _natten(qkv, qkv_ctx, bias):
    b, s_len, _ = qkv.shape
    n_ctx = qkv_ctx.shape[1]
    n_rows = s_len // GRID_W
    tq = NA_ROWS_PER_STEP * GRID_W
    return pl.pallas_call(
        functools.partial(_natten_kernel, n_rows=n_rows),
        grid=(b, 4, n_rows // NA_ROWS_PER_STEP),
        in_specs=[
            pl.BlockSpec((1, tq, LANES), lambda bi, hp, i: (bi, i, hp)),
            pl.BlockSpec((1, s_len, LANES), lambda bi, hp, i: (bi, 0, 4 + hp)),
            pl.BlockSpec((1, s_len, LANES), lambda bi, hp, i: (bi, 0, 8 + hp)),
            pl.BlockSpec((1, n_ctx, LANES), lambda bi, hp, i: (bi, 0, 4 + hp)),
            pl.BlockSpec((1, n_ctx, LANES), lambda bi, hp, i: (bi, 0, 8 + hp)),
            pl.BlockSpec((NA_ROWS, 2, GRID_W, NA_ROWS * GRID_W), lambda bi, hp, i: (0, hp, 0, 0)),
        ],
        out_specs=pl.BlockSpec((1, tq, LANES), lambda bi, hp, i: (bi, i, hp)),
        out_shape=jax.ShapeDtypeStruct((b, s_len, 4 * LANES), BF16),
        compiler_params=_cparams(("parallel", "arbitrary", "arbitrary")),
        name="natten",
    )(qkv, qkv, qkv, qkv_ctx, qkv_ctx, bias)


def _head_sum(x, bd):
    return _dot2_exact_rhs(x, bd)


def _rwkv_prep_kernel(x_ref, xp_ref, xn_ref, mu_ref, w0_ref, w2_ref, a0_ref, a2_ref, g2_ref, kk_ref, bd_ref,
                      rkvk_ref, ld_ref, a_ref, g_ref):
    i = pl.program_id(1)
    n_t = pl.num_programs(1)
    x = x_ref[0]
    tm = x.shape[0]
    prev_row = jnp.where(i > 0, xp_ref[0, 7:8, :], 0.0)
    next_row = jnp.where(i < n_t - 1, xn_ref[0, 0:1, :], 0.0)
    rows = lax.broadcasted_iota(jnp.int32, (tm, 1), 0)
    x_prev = jnp.where(rows == 0, prev_row, pltpu.roll(x, 1, axis=0))
    x_next = jnp.where(rows == tm - 1, next_row, pltpu.roll(x, tm - 1, axis=0))
    xs = x + mu_ref[...] * (0.5 * (x_prev + x_next) - x)
    w = D_WIDTH
    k = xs[:, w:2 * w]
    rkvk_ref[0, :, 0:3 * w] = xs[:, 0:3 * w]
    kk = k * kk_ref[...]
    nrm = jnp.sqrt(jnp.concatenate([_head_sum(kk[:, j * LANES:(j + 1) * LANES] ** 2, bd_ref[...])
                                    for j in range(w // LANES)], axis=1))
    rkvk_ref[0, :, 3 * w:4 * w] = kk / jnp.maximum(nrm, 1e-12)
    wd = xs[:, 3 * w:3 * w + LANES]
    ad = xs[:, 3 * w + LANES:3 * w + 2 * LANES]
    gd = xs[:, 3 * w + 2 * LANES:3 * w + 3 * LANES]
    wraw = -jax.nn.softplus(-(w0_ref[...] + _dot3(jnp.tanh(wd), w2_ref[...]))) - 0.5
    ld_ref[0] = -jnp.exp(wraw)
    a_ref[0] = jax.nn.sigmoid(a0_ref[...] + _dot3(ad, a2_ref[...]))
    g_ref[0] = _dot3(jax.nn.sigmoid(gd), g2_ref[...])


def _rwkv_prep(dcols, mu, w0, w2, a0, a2, g2, k_k, tm):
    b, t, n = dcols.shape
    w = D_WIDTH
    bd = jnp.asarray(np.kron(np.eye(2), np.ones((HEAD_DIM, HEAD_DIM))), BF16)
    pad_rows = lambda m, r0: jnp.zeros((LANES, m.shape[1]), F32).at[r0:r0 + m.shape[0]].set(m)
    w2p = jnp.concatenate([pad_rows(w2[0], 0), pad_rows(w2[1], 32)], axis=1)
    a2p = jnp.concatenate([pad_rows(a2[0], 0), pad_rows(a2[1], 32)], axis=1)
    g2p = pad_rows(g2, 0)
    const = lambda bi, i: (0, 0)
    t8 = tm // 8
    return pl.pallas_call(
        _rwkv_prep_kernel,
        grid=(b, t // tm),
        in_specs=[
            pl.BlockSpec((1, tm, n), lambda bi, i: (bi, i, 0)),
            pl.BlockSpec((1, 8, n), lambda bi, i: (bi, jnp.maximum(i * t8 - 1, 0), 0)),
            pl.BlockSpec((1, 8, n), lambda bi, i: (bi, jnp.minimum((i + 1) * t8, t // 8 - 1), 0)),
            pl.BlockSpec((1, n), const),
            pl.BlockSpec((1, 2 * w), const),
            pl.BlockSpec((LANES, 2 * w), const),
            pl.BlockSpec((1, 2 * w), const),
            pl.BlockSpec((LANES, 2 * w), const),
            pl.BlockSpec((LANES, w), const),
            pl.BlockSpec((1, w), const),
            pl.BlockSpec((LANES, LANES), const),
        ],
        out_specs=[
            pl.BlockSpec((1, tm, 4 * w), lambda bi, i: (bi, i, 0)),
            pl.BlockSpec((1, tm, 2 * w), lambda bi, i: (bi, i, 0)),
            pl.BlockSpec((1, tm, 2 * w), lambda bi, i: (bi, i, 0)),
            pl.BlockSpec((1, tm, w), lambda bi, i: (bi, i, 0)),
        ],
        out_shape=[
            jax.ShapeDtypeStruct((b, t, 4 * w), F32),
            jax.ShapeDtypeStruct((b, t, 2 * w), F32),
            jax.ShapeDtypeStruct((b, t, 2 * w), F32),
            jax.ShapeDtypeStruct((b, t, w), F32),
        ],
        compiler_params=_cparams(("parallel", "arbitrary")),
        name="rwkv_prep",
    )(dcols, dcols, dcols, mu.reshape(1, n), w0.reshape(1, 2 * w), w2p, a0.reshape(1, 2 * w), a2p, g2p,
      k_k.reshape(1, w), bd)


SCAN_CHUNK = 64
SCAN_TILE = 256


def _split3(x):
    hi = x.astype(BF16)
    r1 = x - hi.astype(F32)
    mid = r1.astype(BF16)
    lo = (r1 - mid.astype(F32)).astype(BF16)
    return hi, mid, lo


def _tri_inverse(l_mat, eye, masks):
    m8, m16, m32 = masks
    p1 = l_mat * m8
    p2 = _dot3(p1, p1)
    p4 = _dot3(p2, p2)
    x = eye + p1
    x = x + _dot3(x, p2)
    x = x + _dot3(x, p4)
    for inner, outer in ((m8, m16), (m16, m32), (m32, None)):
        off = l_mat * ((1.0 - inner) if outer is None else (outer - inner))
        x = x + _dot3(_dot3(x, off), x)
    return x


def _rwkv_scan_kernel(rkvk_r_ref, rkvk_k_ref, rkvk_v_ref, rkvk_kk_ref, ld_ref, a_ref, ka_ref, s0_ref,
                      y_ref, sf_ref, state_ref, *, reverse_axis):
    z = pl.program_id(reverse_axis)
    it = pl.program_id(3)
    n_t = pl.num_programs(3)
    c = SCAN_CHUNK
    n_chunks = SCAN_TILE // c

    @pl.when(it == 0)
    def _():
        state_ref[...] = s0_ref[0, 0, 0]

    rev = z == 1
    ri = lax.broadcasted_iota(jnp.int32, (c, c), 0)
    ci = lax.broadcasted_iota(jnp.int32, (c, c), 1)
    before = (ri - ci) * jnp.where(rev, -1, 1) > 0
    strict = before.astype(F32)
    incl = jnp.logical_or(before, ri == ci).astype(F32)
    eye = (ri == ci).astype(F32)
    tri_incl_b = incl.astype(BF16)
    blk = lambda n: ((ri // n) == (ci // n)).astype(F32)
    masks = (blk(8), blk(16), blk(32))
    lane = lax.broadcasted_iota(jnp.int32, (1, LANES), 1)
    head_masks = ((lane < HEAD_DIM).astype(F32), (lane >= HEAD_DIM).astype(F32))
    r2 = lax.broadcasted_iota(jnp.int32, (LANES, LANES), 0)
    c2 = lax.broadcasted_iota(jnp.int32, (LANES, LANES), 1)
    bd128 = ((r2 // HEAD_DIM) == (c2 // HEAD_DIM)).astype(F32)
    ka = ka_ref[...]

    def chunk_terms(j):
        rows = pl.ds(j * c, c)
        r = rkvk_r_ref[0, rows, :]
        k = rkvk_k_ref[0, rows, :]
        v = rkvk_v_ref[0, rows, :]
        kk = rkvk_kk_ref[0, rows, :]
        ld = ld_ref[0, rows, :]
        a = a_ref[0, rows, :]
        h3 = _split3(ld)
        cum = _dot(tri_incl_b, h3[0]) + _dot(tri_incl_b, h3[1]) + _dot(tri_incl_b, h3[2])
        p = jnp.exp(cum)
        pinv = jnp.exp(-cum)
        p_prev = jnp.exp(cum - ld)
        p_end = jnp.exp(jnp.sum(ld, axis=0, keepdims=True))
        r_t = r * p
        a_t = -kk * p_prev
        b_t = kk * a * pinv
        k_t = k * (1.0 + (a - 1.0) * ka) * pinv
        w_pair = jnp.zeros((c, LANES), F32)
        u0 = jnp.zeros((c, LANES), F32)
        y0 = jnp.zeros((c, LANES), F32)
        m_rb = []
        for hm in head_masks:
            lhs = jnp.concatenate([a_t * hm, r_t * hm], axis=0)
            rhs = jnp.concatenate([b_t, k_t], axis=0)
            g = _dot3_nt(lhs, rhs)
            l_ab = g[:c, :c] * strict
            l_ak = g[:c, c:] * strict
            m_rb.append(g[c:, :c] * incl)
            m_rk = g[c:, c:] * incl
            t_inv = _tri_inverse(l_ab, eye, masks)
            vh = v * hm
            w_pair = w_pair + _dot3(t_inv, a_t * hm)
            u0 = u0 + _dot3(t_inv, _dot3(l_ak, vh))
            y0 = y0 + _dot3(m_rk, vh)
        kv = _dot3(v.T, k_t) * bd128
        return r_t, b_t, w_pair, u0, y0, m_rb, kv, p_end

    terms = [chunk_terms(j) for j in range(n_chunks)]

    def seq_step(j):
        r_t, b_t, w_pair, u0, y0, m_rb, kv, p_end = terms[j]
        s = state_ref[...]
        u = _dot3_nt(w_pair, s) + u0
        y = _dot3_nt(r_t, s) + y0
        for hm, m in zip(head_masks, m_rb):
            y = y + _dot3(m, u * hm)
        state_ref[...] = (s + _dot3(u.T, b_t) * bd128 + kv) * p_end
        y_ref[0, 0, pl.ds(j * c, c), :] = y

    @pl.when(jnp.logical_not(rev))
    def _():
        for j in range(n_chunks):
            seq_step(j)

    @pl.when(rev)
    def _():
        for j in reversed(range(n_chunks)):
            seq_step(j)

    @pl.when(it == n_t - 1)
    def _():
        sf_ref[0, 0, 0] = state_ref[...]


def _rwkv_scan(rkvk, ld, a, k_a, s0):
    b, t, _ = rkvk.shape
    n_t = t // SCAN_TILE
    tile = lambda z, i: jnp.where(z == 1, n_t - 1 - i, i)
    col = lambda base: (lambda bi, z, hp, i: (bi, tile(z, i), base + hp))
    return pl.pallas_call(
        functools.partial(_rwkv_scan_kernel, reverse_axis=1),
        grid=(b, 2, 4, n_t),
        in_specs=[
            pl.BlockSpec((1, SCAN_TILE, LANES), col(0)),
            pl.BlockSpec((1, SCAN_TILE, LANES), col(4)),
            pl.BlockSpec((1, SCAN_TILE, LANES), col(8)),
            pl.BlockSpec((1, SCAN_TILE, LANES), col(12)),
            pl.BlockSpec((1, SCAN_TILE, LANES), lambda bi, z, hp, i: (bi, tile(z, i), z * 4 + hp)),
            pl.BlockSpec((1, SCAN_TILE, LANES), lambda bi, z, hp, i: (bi, tile(z, i), z * 4 + hp)),
            pl.BlockSpec((1, LANES), lambda bi, z, hp, i: (0, hp)),
            pl.BlockSpec((1, 1, 1, LANES, LANES), lambda bi, z, hp, i: (bi, z, hp, 0, 0)),
        ],
        out_specs=[
            pl.BlockSpec((1, 1, SCAN_TILE, LANES), lambda bi, z, hp, i: (bi, z, tile(z, i), hp)),
            pl.BlockSpec((1, 1, 1, LANES, LANES), lambda bi, z, hp, i: (bi, z, hp, 0, 0)),
        ],
        out_shape=[
            jax.ShapeDtypeStruct((b, 2, t, D_WIDTH), F32),
            jax.ShapeDtypeStruct((b, 2, 4, LANES, LANES), F32),
        ],
        scratch_shapes=[pltpu.VMEM((LANES, LANES), F32)],
        compiler_params=_cparams(("parallel", "arbitrary", "arbitrary", "arbitrary")),
        name="rwkv_scan",
    )(rkvk, rkvk, rkvk, rkvk, ld, a, k_a.reshape(1, D_WIDTH), s0)


def _rwkv_out_kernel(y_ref, rkvk_ref, a_ref, g_ref, ka_ref, rk_ref, lnw_ref, lnb_ref, bd_ref, o_ref):
    w = D_WIDTH
    bd = bd_ref[...]
    for j in range(w // LANES):
        cs = slice(j * LANES, (j + 1) * LANES)
        y = y_ref[0, 0, :, cs] + y_ref[0, 1, :, cs]
        mean = _head_sum(y, bd) * (1.0 / HEAD_DIM)
        yc = y - mean
        var = _head_sum(yc * yc, bd) * (1.0 / HEAD_DIM)
        yn = yc * lax.rsqrt(var + D_GN_EPS) * lnw_ref[:, cs] + lnb_ref[:, cs]
        r = rkvk_ref[0, :, cs]
        k = rkvk_ref[0, :, w + j * LANES:w + (j + 1) * LANES]
        v = rkvk_ref[0, :, 2 * w + j * LANES:2 * w + (j + 1) * LANES]
        bonus = jnp.zeros_like(y)
        for z in range(2):
            a = a_ref[0, :, z * w + j * LANES:z * w + (j + 1) * LANES]
            k_dir = k * (1.0 + (a - 1.0) * ka_ref[:, cs])
            bonus = bonus + _head_sum(r * k_dir * rk_ref[:, cs], bd) * v
        o_ref[0, :, cs] = ((yn + bonus) * g_ref[0, :, cs]).astype(o_ref.dtype)


def _rwkv_out(y, rkvk, a, g, k_a, r_k, ln_w, ln_b, tm):
    b, _, t, w = y.shape
    bd = jnp.asarray(np.kron(np.eye(2), np.ones((HEAD_DIM, HEAD_DIM))), BF16)
    const = lambda bi, i: (0, 0)
    vec = lambda p: p.reshape(1, w).astype(F32)
    return pl.pallas_call(
        _rwkv_out_kernel,
        grid=(b, t // tm),
        in_specs=[
            pl.BlockSpec((1, 2, tm, w), lambda bi, i: (bi, 0, i, 0)),
            pl.BlockSpec((1, tm, 4 * w), lambda bi, i: (bi, i, 0)),
            pl.BlockSpec((1, tm, 2 * w), lambda bi, i: (bi, i, 0)),
            pl.BlockSpec((1, tm, w), lambda bi, i: (bi, i, 0)),
            pl.BlockSpec((1, w), const),
            pl.BlockSpec((1, w), const),
            pl.BlockSpec((1, w), const),
            pl.BlockSpec((1, w), const),
            pl.BlockSpec((LANES, LANES), const),
        ],
        out_specs=pl.BlockSpec((1, tm, w), lambda bi, i: (bi, i, 0)),
        out_shape=jax.ShapeDtypeStruct((b, t, w), BF16),
        compiler_params=_cparams(("parallel", "arbitrary")),
        name="rwkv_out",
    )(y, rkvk, a, g, vec(k_a), vec(r_k), vec(ln_w), vec(ln_b), bd)


def _mod_vectors(mods_layer, b):
    d = D_MODEL
    lat = [mods_layer[:b, j * d:(j + 1) * d].reshape(b, 1, d) for j in range(6)]
    ctx = [jnp.broadcast_to(mods_layer[b:b + 1, j * d:(j + 1) * d].reshape(1, 1, d), (b, 1, d)) for j in range(6)]
    return lat, ctx


def _tile2(v):
    return jnp.concatenate([v, v]).astype(F32)


def _ab_layer(x_lat, x_ctx, mods_layer, layer, norm_mix, norm_ffn, wg, wu, wd, w_in, w_out,
              a_qn, a_kn, a_lam, a_subln, b_qn, b_kn, b_sink, rope_lat, rope_ctx, need_ctx):
    b, s_len, d = x_lat.shape
    (sh_l, sc_l, g_l, fsh_l, fsc_l, fg_l), (sh_c, sc_c, g_c, fsh_c, fsc_c, fg_c) = _mod_vectors(mods_layer, b)
    lambda_init = 0.8 - 0.6 * math.exp(-0.3 * layer)

    aw = A_HEADS * LANES
    aq, ak, av = w_in[:, :aw], w_in[:, aw:2 * aw], w_in[:, 2 * aw:3 * aw]
    bq = w_in[:, 3 * aw:4 * aw]
    bk = w_in[:, 4 * aw:4 * aw + LANES]
    bv = w_in[:, 4 * aw + LANES:4 * aw + 2 * LANES]
    dup = lambda w: jnp.concatenate([w[:, :64], w[:, :64], w[:, 64:], w[:, 64:]], axis=1)
    w_cat = jnp.concatenate([aq, ak, bq, dup(bk), dup(bv), av], axis=1).astype(BF16)
    scale = HEAD_DIM ** -0.5
    hg = jnp.stack([_tile2(a_qn) * scale, _tile2(a_kn), _tile2(b_qn) * scale, _tile2(b_kn)]
                   + [jnp.zeros((LANES,), F32)] * 4)
    plan = ([("qk", 0, 0, j) for j in range(4)] + [("qk", 1, 0, 4 + j) for j in range(4)]
            + [("qk", 2, 0, 8 + j) for j in range(4)] + [("qk", 3, 0, 12 + j) for j in range(2)]
            + [("v", 0, 0, 14 + j) for j in range(2)] + [("vT", 0, 1, j) for j in range(4)])
    out_defs = [("rows", 16, BF16), ("vT", 4, BF16)]
    qkv_l, avT_l = _inproj(x_lat, sh_l, sc_l, norm_mix, w_cat, rope_lat[0], rope_lat[1], hg, plan, out_defs, tm=512)
    qkv_c, avT_c = _inproj(x_ctx, sh_c, sc_c, norm_mix, w_cat, rope_ctx[0], rope_ctx[1], hg, plan, out_defs,
                           tm=x_ctx.shape[1])

    lam_f = a_lam.astype(F32)
    lam = (jnp.exp(jnp.sum(lam_f[0] * lam_f[1])) - jnp.exp(jnp.sum(lam_f[2] * lam_f[3])) + lambda_init).reshape(1, 1)
    post = 1.0 - lambda_init
    a_lat = _diffattn(lam, qkv_l, 0, qkv_c, 4, avT_c, a_subln, post, lat=(qkv_l, 4, avT_l))
    b_lat = _window_attn(b_sink, qkv_l, 8, 12, 14, qkv_c, with_win=True)
    x_lat = _out_ffn(x_lat, a_lat, b_lat, w_out, g_l, norm_ffn, fsh_l, fsc_l, fg_l, wg, wu, wd, tm=256)
    if need_ctx:
        a_ctx = _diffattn(lam, qkv_c, 0, qkv_c, 4, avT_c, a_subln, post)
        b_ctx = _window_attn(b_sink, qkv_c, 8, 12, 14, qkv_c, with_win=False)
        x_ctx = _out_ffn(x_ctx, a_ctx, b_ctx, w_out, g_c, norm_ffn, fsh_c, fsc_c, fg_c, wg, wu, wd, tm=256)
    return x_lat, x_ctx


def _pad_block(m):
    return jnp.concatenate([m, jnp.zeros(m.shape[:-1] + (LANES - m.shape[-1],), m.dtype)], axis=-1)


def _cd_layer(x_lat, x_ctx, mods_layer, norm_mix, norm_ffn, wg, wu, wd, w_in, w_out, c_qn, c_kn, c_rpb,
              d_mu, d_w0, d_w2, d_a0, d_a2, d_g2, d_k_k, d_k_a, d_r_k, d_ln_w, d_ln_b):
    b, s_len, d = x_lat.shape
    n_ctx = x_ctx.shape[1]
    (sh_l, sc_l, g_l, fsh_l, fsc_l, fg_l), (sh_c, sc_c, _, _, _, _) = _mod_vectors(mods_layer, b)
    w = D_WIDTH
    lora = lambda m: [_pad_block(m[..., 3 * w:3 * w + 64]), _pad_block(m[..., 3 * w + 64:3 * w + 128]),
                      _pad_block(m[..., 3 * w + 128:])]
    w_d = w_in[:, 3 * w:]
    w_cat = jnp.concatenate([w_in[:, :3 * w], w_d[:, :3 * w]] + lora(w_d), axis=1).astype(BF16)
    mu_p = jnp.concatenate([d_mu[:3 * w]] + lora(d_mu))
    scale = HEAD_DIM ** -0.5
    hg = jnp.stack([_tile2(c_qn) * scale, _tile2(c_kn)] + [jnp.zeros((LANES,), F32)] * 6)
    plan = ([("qk", 0, 0, j) for j in range(4)] + [("qk", 1, 0, 4 + j) for j in range(4)]
            + [("v", 0, 0, 8 + j) for j in range(4)] + [("raw", 0, 1, j) for j in range(15)])
    out_defs = [("rows", 12, BF16), ("rows", 15, F32)]
    no_rope = lambda n: (jnp.ones((n, LANES), F32), jnp.zeros((n, LANES), F32))
    qkv_l, dcols_l = _inproj(x_lat, sh_l, sc_l, norm_mix, w_cat, *no_rope(s_len), hg, plan, out_defs, tm=512)
    qkv_c, dcols_c = _inproj(x_ctx, sh_c, sc_c, norm_mix, w_cat, *no_rope(n_ctx), hg, plan, out_defs, tm=n_ctx)

    c_lat = _natten(qkv_l, qkv_c, _natten_bias(c_rpb))

    prep = functools.partial(_rwkv_prep, mu=mu_p, w0=d_w0, w2=d_w2, a0=d_a0, a2=d_a2, g2=d_g2, k_k=d_k_k, tm=256)
    rkvk_c, ld_c, a_c, _ = prep(dcols_c)
    rkvk_l, ld_l, a_l, g_l_gate = prep(dcols_l)
    s0 = jnp.zeros((b, 2, 4, LANES, LANES), F32)
    _, s_ctx = _rwkv_scan(rkvk_c, ld_c, a_c, d_k_a, s0)
    y_lat, _ = _rwkv_scan(rkvk_l, ld_l, a_l, d_k_a, s_ctx)
    d_lat = _rwkv_out(y_lat, rkvk_l, a_l, g_l_gate, d_k_a, d_r_k, d_ln_w, d_ln_b, tm=512)
    return _out_ffn(x_lat, c_lat, d_lat, w_out, g_l, norm_ffn, fsh_l, fsc_l, fg_l, wg, wu, wd, tm=256)


def kernel(x, c, ctx, c_ctx, ada_w, ada_b, norm_mix, norm_ffn, ffn_w_gate, ffn_w_up, ffn_w_down, ab_w_in, ab_w_out, a_q_norm, a_k_norm, a_lambda, a_subln, b_q_norm, b_k_norm, b_sink, cd_w_in, cd_w_out, c_q_norm, c_k_norm, c_rpb, d_mu, d_w0, d_w2, d_a0, d_a2, d_g2, d_k_k, d_k_a, d_r_k, d_ln_w, d_ln_b):
    b, s_len, d = x.shape
    n_ctx = ctx.shape[1]
    cond = jnp.concatenate([c, c_ctx[None, :], jnp.zeros((8 - b - 1, d), F32)], axis=0)
    mods = _adaln(cond, ada_w, ada_b)
    rope_lat = _rope_tables(s_len)
    rope_ctx = (jnp.ones((n_ctx, LANES), F32), jnp.zeros((n_ctx, LANES), F32))
    x_lat, x_ctx = x, ctx
    x_lat, x_ctx = _ab_layer(
        x_lat, x_ctx, mods[0], 0, norm_mix[0], norm_ffn[0], ffn_w_gate[0].astype(BF16), ffn_w_up[0].astype(BF16),
        ffn_w_down[0].astype(BF16), ab_w_in[0], ab_w_out[0].astype(BF16), a_q_norm[0], a_k_norm[0], a_lambda[0],
        a_subln[0], b_q_norm[0], b_k_norm[0], b_sink[0], rope_lat, rope_ctx, need_ctx=True)
    return _cd_layer(
        x_lat, x_ctx, mods[1], norm_mix[1], norm_ffn[1], ffn_w_gate[1].astype(BF16), ffn_w_up[1].astype(BF16),
        ffn_w_down[1].astype(BF16), cd_w_in[0], cd_w_out[0].astype(BF16), c_q_norm[0], c_k_norm[0], c_rpb[0],
        d_mu[0], d_w0[0], d_w2[0], d_a0[0], d_a2[0], d_g2[0], d_k_k[0], d_k_a[0], d_r_k[0], d_ln_w[0], d_ln_b[0])
```

```python
import functools
import math

import jax
import jax.numpy as jnp
import numpy as np
from jax import lax
from jax.experimental import pallas as pl
from jax.experimental.pallas import tpu as pltpu

F32 = jnp.float32
BF16 = jnp.bfloat16

D_MODEL = 1024
GRID_W = 64
HEAD_DIM = 64
ROPE_THETA = 10000.0
NORM_EPS = 1e-6
NEG_INF = -1e30
LANES = 128
A_HEADS = 4
WINDOW = 128
NA_ROWS = 8
NA_COLS = 16
D_HEADS = 8
D_WIDTH = 512
D_GN_EPS = 64e-5
LOG2E = math.log2(math.e)
VMEM_LIMIT = 56 * 1024 * 1024


def _cparams(sem):
    return pltpu.CompilerParams(dimension_semantics=sem, vmem_limit_bytes=VMEM_LIMIT)


def _dot(a, b):
    return jnp.dot(a, b, preferred_element_type=F32)


def _dot_nt(a, b):
    return lax.dot_general(a, b, (((1,), (1,)), ((), ())), preferred_element_type=F32)


def _split(x):
    hi = x.astype(BF16)
    lo = (x - hi.astype(F32)).astype(BF16)
    return hi, lo


def _dot3(a, b):
    ah, al = _split(a)
    bh, bl = _split(b)
    return _dot(ah, bh) + _dot(al, bh) + _dot(ah, bl)


def _dot3_nt(a, b):
    ah, al = _split(a)
    bh, bl = _split(b)
    return _dot_nt(ah, bh) + _dot_nt(al, bh) + _dot_nt(ah, bl)


def _dot2_exact_rhs(a, b_bf16):
    ah, al = _split(a)
    return _dot(ah, b_bf16) + _dot(al, b_bf16)


def _norm_mod(x, gain, shift, scale):
    ms = jnp.mean(x * x, axis=-1, keepdims=True)
    y = x * lax.rsqrt(ms + NORM_EPS) * gain
    return y * (1.0 + scale) + shift


def _adaln_kernel(c_ref, w_ref, b_ref, o_ref):
    c = c_ref[...]
    o_ref[0] = _dot3(jax.nn.silu(c), w_ref[0]) + b_ref[0]


def _adaln(cond, ada_w, ada_b):
    depth, d, n = ada_w.shape
    tn = 1536
    return pl.pallas_call(
        _adaln_kernel,
        grid=(depth, n // tn),
        in_specs=[
            pl.BlockSpec((8, d), lambda l, j: (0, 0)),
            pl.BlockSpec((1, d, tn), lambda l, j: (l, 0, j)),
            pl.BlockSpec((1, 1, tn), lambda l, j: (l, 0, j)),
        ],
        out_specs=pl.BlockSpec((1, 8, tn), lambda l, j: (l, 0, j)),
        out_shape=jax.ShapeDtypeStruct((depth, 8, n), F32),
        compiler_params=_cparams(("arbitrary", "arbitrary")),
        name="adaln",
    )(cond, ada_w, ada_b.reshape(depth, 1, n))


def _head_norm_rope(y, bd, hg, cos, sin):
    ss = _dot2_exact_rhs(y * y, bd)
    y = y * lax.rsqrt(ss * (1.0 / HEAD_DIM) + NORM_EPS) * hg
    lane = lax.broadcasted_iota(jnp.int32, y.shape, 1)
    first_half = (lane % 32) < 16
    partner = jnp.where(first_half, pltpu.roll(y, LANES - 16, axis=1), pltpu.roll(y, 16, axis=1))
    return y * cos + partner * sin


def _inproj_kernel(x_ref, sh_ref, sc_ref, gain_ref, w_ref, cos_ref, sin_ref, hg_ref, bd_ref, *out_refs, plan):
    h = _norm_mod(x_ref[0], gain_ref[...], sh_ref[0], sc_ref[0]).astype(BF16)
    bd = bd_ref[...]
    cos = cos_ref[...]
    sin = sin_ref[...]
    n_blocks = len(plan)
    for j0 in range(0, n_blocks, 2):
        width = min(2, n_blocks - j0) * LANES
        y2 = _dot(h, w_ref[:, j0 * LANES:j0 * LANES + width])
        for jj in range(width // LANES):
            kind, hg_idx, out_idx, out_blk = plan[j0 + jj]
            y = y2[:, jj * LANES:(jj + 1) * LANES]
            o_ref = out_refs[out_idx]
            if kind == "qk":
                y = _head_norm_rope(y, bd, hg_ref[hg_idx:hg_idx + 1, :], cos, sin)
                o_ref[0, :, out_blk * LANES:(out_blk + 1) * LANES] = y.astype(o_ref.dtype)
            elif kind == "vT":
                o_ref[0, out_blk] = y.T.astype(o_ref.dtype)
            else:
                o_ref[0, :, out_blk * LANES:(out_blk + 1) * LANES] = y.astype(o_ref.dtype)


def _inproj(x, shift, scale, gain, w, cos, sin, hg, plan, out_defs, tm):
    b, t, d = x.shape
    n = w.shape[1]
    bd = jnp.asarray(np.kron(np.eye(2), np.ones((HEAD_DIM, HEAD_DIM))), BF16)
    out_shapes, out_specs = [], []
    for kind, nblk, dt in out_defs:
        if kind == "rows":
            out_shapes.append(jax.ShapeDtypeStruct((b, t, nblk * LANES), dt))
            out_specs.append(pl.BlockSpec((1, tm, nblk * LANES), lambda bi, i: (bi, i, 0)))
        else:
            out_shapes.append(jax.ShapeDtypeStruct((b, nblk, LANES, t), dt))
            out_specs.append(pl.BlockSpec((1, nblk, LANES, tm), lambda bi, i: (bi, 0, 0, i)))
    return pl.pallas_call(
        functools.partial(_inproj_kernel, plan=tuple(plan)),
        grid=(b, t // tm),
        in_specs=[
            pl.BlockSpec((1, tm, d), lambda bi, i: (bi, i, 0)),
            pl.BlockSpec((1, 1, d), lambda bi, i: (bi, 0, 0)),
            pl.BlockSpec((1, 1, d), lambda bi, i: (bi, 0, 0)),
            pl.BlockSpec((1, d), lambda bi, i: (0, 0)),
            pl.BlockSpec((d, n), lambda bi, i: (0, 0)),
            pl.BlockSpec((tm, LANES), lambda bi, i: (i, 0)),
            pl.BlockSpec((tm, LANES), lambda bi, i: (i, 0)),
            pl.BlockSpec(hg.shape, lambda bi, i: (0, 0)),
            pl.BlockSpec((LANES, LANES), lambda bi, i: (0, 0)),
        ],
        out_specs=out_specs,
        out_shape=out_shapes,
        compiler_params=_cparams(("parallel", "arbitrary")),
        name="inproj",
    )(x, shift, scale, gain.reshape(1, d), w, cos, sin, hg, bd)


def _rope_tables(n_tokens):
    axis_dim = HEAD_DIM // 2
    freqs = ROPE_THETA ** (-jnp.arange(0, axis_dim, 2, dtype=F32) / axis_dim)
    t = jnp.arange(n_tokens, dtype=jnp.int32)
    lane = np.arange(LANES)
    d = lane % HEAD_DIM
    use_col = (d // 32) == 1
    f_idx = (d % 32) % 16
    sign = np.where((d % 32) < 16, -1.0, 1.0).astype(np.float32)
    pos = jnp.where(use_col[None, :], (t % GRID_W)[:, None], (t // GRID_W)[:, None]).astype(F32)
    ang = pos * freqs[f_idx][None, :]
    return jnp.cos(ang), jnp.sin(ang) * sign[None, :]


def _diffattn_kernel(lam_ref, q_ref, k_ref, vT_ref, subln_ref, o_ref, s0_ref, s1_ref, *, tk, n_tiles, post_scale):
    q = q_ref[0]
    tq = q.shape[0]
    lane = lax.broadcasted_iota(jnp.int32, q.shape, 1)
    zero = jnp.zeros_like(q)
    qq = jnp.concatenate([jnp.where(lane < HEAD_DIM, q, zero), jnp.where(lane >= HEAD_DIM, q, zero)], axis=0)

    def scores(i, s_ref):
        off = pl.multiple_of(i * tk, tk)
        s = _dot_nt(k_ref[0, pl.ds(off, tk), :], qq)
        s_ref[...] = s
        return jnp.max(s, axis=0, keepdims=True)

    def absorb(i, s_ref, cmax, carry):
        m, l, acc = carry
        off = pl.multiple_of(i * tk, tk)
        m_new = jnp.maximum(m, cmax)
        alpha = jnp.exp2(m - m_new)
        p = jnp.exp2(s_ref[...] - m_new)
        l = alpha * l + jnp.sum(p, axis=0, keepdims=True)
        acc = alpha * acc + _dot(vT_ref[0, 0, :, pl.ds(off, tk)], p.astype(BF16))
        return m_new, l, acc

    carry = (jnp.full((1, 2 * tq), -jnp.inf, F32), jnp.zeros((1, 2 * tq), F32), jnp.zeros((LANES, 2 * tq), F32))
    bufs = (s0_ref, s1_ref)
    cmax = scores(0, bufs[0])
    for i in range(n_tiles):
        cmax_next = scores(i + 1, bufs[(i + 1) % 2]) if i + 1 < n_tiles else None
        carry = absorb(i, bufs[i % 2], cmax, carry)
        cmax = cmax_next
    _, l, acc = carry
    o = acc / l
    oT = o[:, :tq] - lam_ref[0, 0] * o[:, tq:]
    ms = jnp.mean(oT * oT, axis=0, keepdims=True)
    oT = oT * lax.rsqrt(ms + NORM_EPS) * subln_ref[...] * post_scale
    o_ref[0] = oT.T.astype(o_ref.dtype)


DIFF_KEY_TILE = 1280


def _diffattn(lam, q_arr, q_blk0, k_arr, k_blk0, vT, subln, post_scale, tq=128):
    b, t_q, _ = q_arr.shape
    t_k = k_arr.shape[1]
    tk = DIFF_KEY_TILE if (t_k % DIFF_KEY_TILE == 0 and t_k > DIFF_KEY_TILE) else 256
    return pl.pallas_call(
        functools.partial(_diffattn_kernel, tk=tk, n_tiles=t_k // tk, post_scale=post_scale),
        grid=(b, A_HEADS, t_q // tq),
        in_specs=[
            pl.BlockSpec(memory_space=pltpu.SMEM),
            pl.BlockSpec((1, tq, LANES), lambda bi, h, i: (bi, i, q_blk0 + h)),
            pl.BlockSpec((1, t_k, LANES), lambda bi, h, i: (bi, 0, k_blk0 + h)),
            pl.BlockSpec((1, 1, LANES, t_k), lambda bi, h, i: (bi, h, 0, 0)),
            pl.BlockSpec((LANES, 1), lambda bi, h, i: (0, 0)),
        ],
        out_specs=pl.BlockSpec((1, tq, LANES), lambda bi, h, i: (bi, i, h)),
        out_shape=jax.ShapeDtypeStruct((b, t_q, A_HEADS * LANES), BF16),
        scratch_shapes=[pltpu.VMEM((tk, 2 * tq), F32), pltpu.VMEM((tk, 2 * tq), F32)],
        compiler_params=_cparams(("parallel", "arbitrary", "arbitrary")),
        name="diffattn",
    )(lam, q_arr, k_arr, vT, subln.reshape(LANES, 1))


def _window_kernel(sink_ref, q_ref, kc_ref, vc_ref, *rest, with_win, n_blocks):
    if with_win:
        kp_ref, k0_ref, kn_ref, vp_ref, v0_ref, vn_ref, o_ref = rest
    else:
        (o_ref,) = rest
    i = pl.program_id(1)
    tq = q_ref.shape[1]
    n_ctx = kc_ref.shape[1]
    lane = lax.broadcasted_iota(jnp.int32, (1, LANES), 1)
    lo = lane < HEAD_DIM
    if with_win:
        r = lax.broadcasted_iota(jnp.int32, (tq, tq), 0)
        c = lax.broadcasted_iota(jnp.int32, (tq, tq), 1)
        ok_prev = jnp.logical_and(c >= r, i > 0)
        ok_next = jnp.logical_and(c <= r, i < n_blocks - 1)
        valid = jnp.concatenate([ok_prev, jnp.ones((tq, tq), jnp.bool_), ok_next,
                                 jnp.ones((tq, n_ctx), jnp.bool_)], axis=1)
    scores, vcats = [], []
    for g in range(2):
        gs = slice(g * LANES, (g + 1) * LANES)
        if with_win:
            kcat = jnp.concatenate([kp_ref[0, :, gs], k0_ref[0, :, gs], kn_ref[0, :, gs], kc_ref[0, :, gs]], axis=0)
            vcat = jnp.concatenate([vp_ref[0, :, gs], v0_ref[0, :, gs], vn_ref[0, :, gs], vc_ref[0, :, gs]], axis=0)
        else:
            kcat = kc_ref[0, :, gs]
            vcat = vc_ref[0, :, gs]
        q_rows = []
        for cb in range(2):
            qb = q_ref[0, :, (g * 2 + cb) * LANES:(g * 2 + cb + 1) * LANES]
            zq = jnp.zeros_like(qb)
            q_rows += [jnp.where(lo, qb, zq), jnp.where(lo, zq, qb)]
        scores.append(_dot_nt(jnp.concatenate(q_rows, axis=0), kcat))
        vcats.append(vcat)
    for g in range(2):
        s = scores[g]
        if with_win:
            s = jnp.where(jnp.concatenate([valid] * 4, axis=0), s, NEG_INF)
        snk = jnp.concatenate([jnp.broadcast_to(sink_ref[g * 4 + h:g * 4 + h + 1, 0:1], (tq, 1)) for h in range(4)],
                              axis=0)
        m = jnp.maximum(jnp.max(s, axis=-1, keepdims=True), snk)
        e = jnp.exp(s - m)
        denom = jnp.sum(e, axis=-1, keepdims=True) + jnp.exp(snk - m)
        o = _dot(e.astype(BF16), vcats[g]) / denom
        for cb in range(2):
            out = jnp.where(lo, o[(2 * cb) * tq:(2 * cb + 1) * tq], o[(2 * cb + 1) * tq:(2 * cb + 2) * tq])
            o_ref[0, :, (g * 2 + cb) * LANES:(g * 2 + cb + 1) * LANES] = out.astype(o_ref.dtype)


def _window_attn(sink, q_arr, q_blk0, k_blk0, v_blk0, ctx_arr, with_win):
    b, t_q, _ = q_arr.shape
    n_ctx = ctx_arr.shape[1]
    tq = WINDOW
    nb = t_q // tq
    sink_tab = jnp.broadcast_to(sink.astype(F32).reshape(8, 1), (8, LANES))
    in_specs = [
        pl.BlockSpec((8, LANES), lambda bi, i: (0, 0)),
        pl.BlockSpec((1, tq, 4 * LANES), lambda bi, i: (bi, i, q_blk0 // 4)),
        pl.BlockSpec((1, n_ctx, 2 * LANES), lambda bi, i: (bi, 0, k_blk0 // 2)),
        pl.BlockSpec((1, n_ctx, 2 * LANES), lambda bi, i: (bi, 0, v_blk0 // 2)),
    ]
    args = [sink_tab, q_arr, ctx_arr, ctx_arr]
    if with_win:
        for blk0 in (k_blk0, v_blk0):
            in_specs += [
                pl.BlockSpec((1, tq, 2 * LANES), lambda bi, i, c=blk0 // 2: (bi, jnp.maximum(i - 1, 0), c)),
                pl.BlockSpec((1, tq, 2 * LANES), lambda bi, i, c=blk0 // 2: (bi, i, c)),
                pl.BlockSpec((1, tq, 2 * LANES), lambda bi, i, c=blk0 // 2: (bi, jnp.minimum(i + 1, nb - 1), c)),
            ]
            args += [q_arr, q_arr, q_arr]
    return pl.pallas_call(
        functools.partial(_window_kernel, with_win=with_win, n_blocks=nb),
        grid=(b, nb),
        in_specs=in_specs,
        out_specs=pl.BlockSpec((1, tq, 4 * LANES), lambda bi, i: (bi, i, 0)),
        out_shape=jax.ShapeDtypeStruct((b, t_q, 4 * LANES), BF16),
        compiler_params=_cparams(("parallel", "arbitrary")),
        name="window_attn",
    )(*args)


def _out_ffn_kernel(x_ref, a_ref, b_ref, woa_ref, wob_ref, gate_ref, ng_ref, fsh_ref, fsc_ref, fg_ref,
                    wg_ref, wu_ref, wd_ref, o_ref):
    o = _dot(a_ref[0], woa_ref[...]) + _dot(b_ref[0], wob_ref[...])
    x1 = x_ref[0] + gate_ref[0] * o
    h = _norm_mod(x1, ng_ref[...], fsh_ref[0], fsc_ref[0]).astype(BF16)
    act = (jax.nn.silu(_dot(h, wg_ref[...])) * _dot(h, wu_ref[...])).astype(BF16)
    o_ref[0] = x1 + fg_ref[0] * _dot(act, wd_ref[...])


def _out_ffn(x, mix_a, mix_b, wo, gate, ng, fsh, fsc, fg, wg, wu, wd, tm):
    b, t, d = x.shape
    na, nb_ = mix_a.shape[-1], mix_b.shape[-1]
    dff = wg.shape[1]
    row = lambda bi, i: (bi, i, 0)
    vec = lambda bi, i: (bi, 0, 0)
    const = lambda bi, i: (0, 0)
    resident = functools.partial(pl.BlockSpec, index_map=const, pipeline_mode=pl.Buffered(1))
    return pl.pallas_call(
        _out_ffn_kernel,
        grid=(b, t // tm),
        in_specs=[
            pl.BlockSpec((1, tm, d), row),
            pl.BlockSpec((1, tm, na), row),
            pl.BlockSpec((1, tm, nb_), row),
            resident((na, d)),
            resident((nb_, d)),
            pl.BlockSpec((1, 1, d), vec),
            resident((1, d)),
            pl.BlockSpec((1, 1, d), vec),
            pl.BlockSpec((1, 1, d), vec),
            pl.BlockSpec((1, 1, d), vec),
            resident((d, dff)),
            resident((d, dff)),
            resident((dff, d)),
        ],
        out_specs=pl.BlockSpec((1, tm, d), row),
        out_shape=jax.ShapeDtypeStruct((b, t, d), F32),
        compiler_params=_cparams(("parallel", "arbitrary")),
        name="out_ffn",
    )(x, mix_a, mix_b, wo[:na], wo[na:], gate, ng.reshape(1, d), fsh, fsc, fg, wg, wu, wd)


NA_ROWS_PER_STEP = 8


def _natten_kernel(q_ref, k_ref, v_ref, kc_ref, vc_ref, bias_ref, o_ref, *, n_rows):
    step = pl.program_id(2)
    n_win = NA_ROWS * GRID_W
    lane = lax.broadcasted_iota(jnp.int32, (1, LANES), 1)
    lo = lane < HEAD_DIM
    keeps = (lo, jnp.logical_not(lo))
    q_all = q_ref[0]
    zq = jnp.zeros_like(q_all)
    q_half = [jnp.where(keep, q_all, zq) for keep in keeps]
    s_ctx = [_dot_nt(qh, kc_ref[0]) for qh in q_half]
    row_info, s_win = [], []
    for rr in range(NA_ROWS_PER_STEP):
        i = step * NA_ROWS_PER_STEP + rr
        r0 = jnp.clip(i - NA_ROWS // 2, 0, n_rows - NA_ROWS)
        koff = pl.multiple_of(r0 * GRID_W, GRID_W)
        row_info.append((i - r0, koff))
        k_win = k_ref[0, pl.ds(koff, n_win), :]
        for half in range(2):
            s_win.append(_dot_nt(q_half[half][rr * GRID_W:(rr + 1) * GRID_W], k_win))
    e_win, e_ctx, denom = [], [[], []], [[], []]
    for rr in range(NA_ROWS_PER_STEP):
        qs = slice(rr * GRID_W, (rr + 1) * GRID_W)
        for half in range(2):
            sw = s_win[2 * rr + half] + bias_ref[row_info[rr][0], half]
            sc = s_ctx[half][qs]
            m = jnp.maximum(jnp.max(sw, axis=-1, keepdims=True), jnp.max(sc, axis=-1, keepdims=True))
            ew, ec = jnp.exp(sw - m), jnp.exp(sc - m)
            e_win.append(ew.astype(BF16))
            e_ctx[half].append(ec.astype(BF16))
            denom[half].append(jnp.sum(ew, axis=-1, keepdims=True) + jnp.sum(ec, axis=-1, keepdims=True))
    vc = vc_ref[0]
    zc = jnp.zeros_like(vc)
    o_ctx = [_dot(jnp.concatenate(e_ctx[half], axis=0), jnp.where(keeps[half], vc, zc)) for half in range(2)]
    for rr in range(NA_ROWS_PER_STEP):
        qs = slice(rr * GRID_W, (rr + 1) * GRID_W)
        v_win = v_ref[0, pl.ds(row_info[rr][1], n_win), :]
        zv = jnp.zeros_like(v_win)
        out = None
        for half in range(2):
            o = (_dot(e_win[2 * rr + half], jnp.where(keeps[half], v_win, zv)) + o_ctx[half][qs]) / denom[half][rr]
            out = o if out is None else out + o
        o_ref[0, qs, :] = out.astype(o_ref.dtype)


def _natten_bias(rpb):
    n_heads = rpb.shape[0]
    qj = np.arange(GRID_W)[:, None]
    kj = np.arange(GRID_W)[None, :]
    cstart = np.clip(qj - NA_COLS // 2, 0, GRID_W - NA_COLS)
    in_win = (kj >= cstart) & (kj < cstart + NA_COLS)
    pad = GRID_W - NA_COLS
    rpb_p = jnp.pad(rpb.astype(F32), ((0, 0), (0, 0), (pad, pad)))
    toep = jnp.stack([rpb_p[:, :, pad + NA_COLS - 1 - q:pad + NA_COLS - 1 - q + GRID_W] for q in range(GRID_W)],
                     axis=2)
    toep = jnp.where(in_win[None, None], toep, NEG_INF)
    per_class = [jnp.moveaxis(toep[:, NA_ROWS - 1 - d:2 * NA_ROWS - 1 - d], 1, 2) for d in range(NA_ROWS)]
    return jnp.stack(per_class).reshape(NA_ROWS, n_heads, GRID_W, NA_ROWS * GRID_W)


def _natten(qkv, qkv_ctx, bias):
    b, s_len, _ = qkv.shape
    n_ctx = qkv_ctx.shape[1]
    n_rows = s_len // GRID_W
    tq = NA_ROWS_PER_STEP * GRID_W
    return pl.pallas_call(
        functools.partial(_natten_kernel, n_rows=n_rows),
        grid=(b, 4, n_rows // NA_ROWS_PER_STEP),
        in_specs=[
            pl.BlockSpec((1, tq, LANES), lambda bi, hp, i: (bi, i, hp)),
            pl.BlockSpec((1, s_len, LANES), lambda bi, hp, i: (bi, 0, 4 + hp)),
            pl.BlockSpec((1, s_len, LANES), lambda bi, hp, i: (bi, 0, 8 + hp)),
            pl.BlockSpec((1, n_ctx, LANES), lambda bi, hp, i: (bi, 0, 4 + hp)),
            pl.BlockSpec((1, n_ctx, LANES), lambda bi, hp, i: (bi, 0, 8 + hp)),
            pl.BlockSpec((NA_ROWS, 2, GRID_W, NA_ROWS * GRID_W), lambda bi, hp, i: (0, hp, 0, 0)),
        ],
        out_specs=pl.BlockSpec((1, tq, LANES), lambda bi, hp, i: (bi, i, hp)),
        out_shape=jax.ShapeDtypeStruct((b, s_len, 4 * LANES), BF16),
        compiler_params=_cparams(("parallel", "arbitrary", "arbitrary")),
        name="natten",
    )(qkv, qkv, qkv, qkv_ctx, qkv_ctx, bias)


def _head_sum(x, bd):
    return _dot2_exact_rhs(x, bd)


def _rwkv_prep_kernel(x_ref, xp_ref, xn_ref, mu_ref, w0_ref, w2_ref, a0_ref, a2_ref, g2_ref, kk_ref, bd_ref,
                      rkvk_ref, ld_ref, a_ref, g_ref):
    i = pl.program_id(1)
    n_t = pl.num_programs(1)
    x = x_ref[0]
    tm = x.shape[0]
    prev_row = jnp.where(i > 0, xp_ref[0, 7:8, :], 0.0)
    next_row = jnp.where(i < n_t - 1, xn_ref[0, 0:1, :], 0.0)
    rows = lax.broadcasted_iota(jnp.int32, (tm, 1), 0)
    x_prev = jnp.where(rows == 0, prev_row, pltpu.roll(x, 1, axis=0))
    x_next = jnp.where(rows == tm - 1, next_row, pltpu.roll(x, tm - 1, axis=0))
    xs = x + mu_ref[...] * (0.5 * (x_prev + x_next) - x)
    w = D_WIDTH
    k = xs[:, w:2 * w]
    rkvk_ref[0, :, 0:3 * w] = xs[:, 0:3 * w]
    kk = k * kk_ref[...]
    nrm = jnp.sqrt(jnp.concatenate([_head_sum(kk[:, j * LANES:(j + 1) * LANES] ** 2, bd_ref[...])
                                    for j in range(w // LANES)], axis=1))
    rkvk_ref[0, :, 3 * w:4 * w] = kk / jnp.maximum(nrm, 1e-12)
    wd = xs[:, 3 * w:3 * w + LANES]
    ad = xs[:, 3 * w + LANES:3 * w + 2 * LANES]
    gd = xs[:, 3 * w + 2 * LANES:3 * w + 3 * LANES]
    wraw = -jax.nn.softplus(-(w0_ref[...] + _dot3(jnp.tanh(wd), w2_ref[...]))) - 0.5
    ld_ref[0] = -jnp.exp(wraw)
    a_ref[0] = jax.nn.sigmoid(a0_ref[...] + _dot3(ad, a2_ref[...]))
    g_ref[0] = _dot3(jax.nn.sigmoid(gd), g2_ref[...])


def _rwkv_prep(dcols, mu, w0, w2, a0, a2, g2, k_k, tm):
    b, t, n = dcols.shape
    w = D_WIDTH
    bd = jnp.asarray(np.kron(np.eye(2), np.ones((HEAD_DIM, HEAD_DIM))), BF16)
    pad_rows = lambda m, r0: jnp.zeros((LANES, m.shape[1]), F32).at[r0:r0 + m.shape[0]].set(m)
    w2p = jnp.concatenate([pad_rows(w2[0], 0), pad_rows(w2[1], 32)], axis=1)
    a2p = jnp.concatenate([pad_rows(a2[0], 0), pad_rows(a2[1], 32)], axis=1)
    g2p = pad_rows(g2, 0)
    const = lambda bi, i: (0, 0)
    t8 = tm // 8
    return pl.pallas_call(
        _rwkv_prep_kernel,
        grid=(b, t // tm),
        in_specs=[
            pl.BlockSpec((1, tm, n), lambda bi, i: (bi, i, 0)),
            pl.BlockSpec((1, 8, n), lambda bi, i: (bi, jnp.maximum(i * t8 - 1, 0), 0)),
            pl.BlockSpec((1, 8, n), lambda bi, i: (bi, jnp.minimum((i + 1) * t8, t // 8 - 1), 0)),
            pl.BlockSpec((1, n), const),
            pl.BlockSpec((1, 2 * w), const),
            pl.BlockSpec((LANES, 2 * w), const),
            pl.BlockSpec((1, 2 * w), const),
            pl.BlockSpec((LANES, 2 * w), const),
            pl.BlockSpec((LANES, w), const),
            pl.BlockSpec((1, w), const),
            pl.BlockSpec((LANES, LANES), const),
        ],
        out_specs=[
            pl.BlockSpec((1, tm, 4 * w), lambda bi, i: (bi, i, 0)),
            pl.BlockSpec((1, tm, 2 * w), lambda bi, i: (bi, i, 0)),
            pl.BlockSpec((1, tm, 2 * w), lambda bi, i: (bi, i, 0)),
            pl.BlockSpec((1, tm, w), lambda bi, i: (bi, i, 0)),
        ],
        out_shape=[
            jax.ShapeDtypeStruct((b, t, 4 * w), F32),
            jax.ShapeDtypeStruct((b, t, 2 * w), F32),
            jax.ShapeDtypeStruct((b, t, 2 * w), F32),
            jax.ShapeDtypeStruct((b, t, w), F32),
        ],
        compiler_params=_cparams(("parallel", "arbitrary")),
        name="rwkv_prep",
    )(dcols, dcols, dcols, mu.reshape(1, n), w0.reshape(1, 2 * w), w2p, a0.reshape(1, 2 * w), a2p, g2p,
      k_k.reshape(1, w), bd)


SCAN_CHUNK = 64
SCAN_TILE = 512


def _split3(x):
    hi = x.astype(BF16)
    r1 = x - hi.astype(F32)
    mid = r1.astype(BF16)
    lo = (r1 - mid.astype(F32)).astype(BF16)
    return hi, mid, lo


def _mm(a, b, precise):
    return _dot3(a, b) if precise else _dot(a.astype(BF16), b.astype(BF16))


def _mm_nt(a, b, precise):
    return _dot3_nt(a, b) if precise else _dot_nt(a.astype(BF16), b.astype(BF16))


def _blockwise_mm(x, y, bd_mask, precise):
    n_blk = x.shape[1] // SCAN_CHUNK
    expand = lambda m: jnp.where(bd_mask, jnp.concatenate([m] * n_blk, axis=0), jnp.zeros((), m.dtype))
    if not precise:
        return _dot(x.astype(BF16), expand(y.astype(BF16)))
    xh, xl = _split(x)
    yh, yl = _split(y)
    ybd = expand(yh)
    return _dot(xh, ybd) + _dot(xl, ybd) + _dot(xh, expand(yl))


SCAN_PRECISE_INVERSE = True
SCAN_PRECISE_STATE = True
SCAN_PRECISE_OTHER = False


def _rwkv_scan_kernel(*refs, n_chunks):
    fwd_refs, rev_refs = refs[0:6], refs[6:12]
    ka_ref, s0_ref = refs[12:14]
    y_refs = refs[14:16]
    sf_ref, state_ref = refs[16:18]
    it = pl.program_id(2)
    n_t = pl.num_programs(2)
    c = SCAN_CHUNK
    n_units = n_chunks // 2

    @pl.when(it == 0)
    def _():
        state_ref[...] = s0_ref[0, :, 0]

    ri = lax.broadcasted_iota(jnp.int32, (c, c), 0)
    ci = lax.broadcasted_iota(jnp.int32, (c, c), 1)
    eye = (ri == ci).astype(F32)
    tile_l = lambda m, n: jnp.concatenate([m] * n, axis=1)
    eye4 = tile_l(eye, 4)
    blk = lambda n: ((ri // n) == (ci // n)).astype(F32)
    m8, m16, m32 = tile_l(blk(8), 4), tile_l(blk(16), 4), tile_l(blk(32), 4)
    r4 = lax.broadcasted_iota(jnp.int32, (4 * c, 4 * c), 0)
    c4 = lax.broadcasted_iota(jnp.int32, (4 * c, 4 * c), 1)
    bd4 = (r4 // c) == (c4 // c)
    lane = lax.broadcasted_iota(jnp.int32, (1, LANES), 1)
    hm_lo = (lane < HEAD_DIM).astype(F32)
    hm_hi = 1.0 - hm_lo
    r2 = lax.broadcasted_iota(jnp.int32, (LANES, LANES), 0)
    c2 = lax.broadcasted_iota(jnp.int32, (LANES, LANES), 1)
    bd128 = ((r2 // HEAD_DIM) == (c2 // HEAD_DIM)).astype(F32)
    eye128 = (r2 == c2).astype(F32)
    ka = ka_ref[...]
    vstack = lambda *xs: jnp.concatenate(xs, axis=0)
    hstack = lambda *xs: jnp.concatenate(xs, axis=1)
    split_heads = lambda m: vstack(m * hm_lo, m * hm_hi)
    po = SCAN_PRECISE_OTHER

    rows = [slice(j * c, (j + 1) * c) for j in range(n_chunks)]
    masks = []
    for z in range(2):
        strict = ((ci > ri) if z == 1 else (ci < ri)).astype(F32)
        incl = strict + eye
        masks.append(dict(strict2=tile_l(strict, 2), incl2=tile_l(incl, 2), tri_b=incl.astype(BF16)))

    cums = []
    for z, in_refs in enumerate((fwd_refs, rev_refs)):
        h3 = _split3(hstack(*[in_refs[4][0, rw, :] for rw in rows]))
        tri_b = masks[z]["tri_b"]
        cums.append(_dot(tri_b, h3[0]) + _dot(tri_b, h3[1]) + _dot(tri_b, h3[2]))

    chunks = []
    for z, in_refs in enumerate((fwd_refs, rev_refs)):
        r_ref, k_ref, v_ref, kk_ref, ld_ref, a_ref = in_refs
        for j, rw in enumerate(rows):
            r, k, v, kk = r_ref[0, rw, :], k_ref[0, rw, :], v_ref[0, rw, :], kk_ref[0, rw, :]
            ld, a = ld_ref[0, rw, :], a_ref[0, rw, :]
            cum = cums[z][:, j * LANES:(j + 1) * LANES]
            pinv = jnp.exp(-cum)
            p_end = jnp.exp(cum[0:1, :] if z == 1 else cum[c - 1:c, :])
            r_t = r * jnp.exp(cum)
            a_t = -kk * jnp.exp(cum - ld)
            b_t = kk * a * pinv
            k_t = k * (1.0 + (a - 1.0) * ka) * pinv
            chunks.append(dict(z=z, j=j, r_t=r_t, a_t=a_t, b_t=b_t, k_t=k_t, v=v, p_end=p_end))
    for ch in chunks:
        g = _mm_nt(vstack(ch["a_t"], ch["r_t"]), vstack(split_heads(ch["b_t"]), split_heads(ch["k_t"])), po)
        mk = masks[ch["z"]]
        ch.update(l_ab=g[:c, :2 * c] * mk["strict2"], l_ak=g[:c, 2 * c:] * mk["strict2"],
                  m_rb=g[c:, :2 * c] * mk["incl2"], m_rk=g[c:, 2 * c:] * mk["incl2"])
    for ch in chunks:
        lv_y0 = _mm(vstack(ch["l_ak"], ch["m_rk"]), split_heads(ch["v"]), po)
        ch.update(lv=lv_y0[:c], y0=lv_y0[c:])

    mm4 = lambda x, y: _blockwise_mm(x, y, bd4, SCAN_PRECISE_INVERSE)
    l4s = [hstack(chunks[2 * u]["l_ab"], chunks[2 * u + 1]["l_ab"]) for u in range(2 * n_units)]
    p1s = [l4 * m8 for l4 in l4s]
    p2s = [mm4(p1, p1) for p1 in p1s]
    xs = [eye4 + p1 for p1 in p1s]
    xs = [x + mm4(x, p2) for x, p2 in zip(xs, p2s)]
    p4s = [mm4(p2, p2) for p2 in p2s]
    xs = [x + mm4(x, p4) for x, p4 in zip(xs, p4s)]
    for inner, outer in ((m8, m16), (m16, m32), (m32, None)):
        sel = (1.0 - inner) if outer is None else (outer - inner)
        ys = [mm4(x, l4 * sel) for x, l4 in zip(xs, l4s)]
        xs = [x + mm4(y, x) for x, y in zip(xs, ys)]

    for i, ch in enumerate(chunks):
        t_inv = xs[i // 2][:, (i % 2) * LANES:(i % 2 + 1) * LANES]
        w_u0 = _mm(t_inv, hstack(split_heads(ch["a_t"]), split_heads(ch["lv"])), po)
        ch.update(w=w_u0[:, :LANES], u0=w_u0[:, LANES:])
    for ch in chunks:
        m = _mm(ch["w"].T, ch["b_t"], po)
        ch.update(a_mat=(eye128 + m * bd128) * ch["p_end"])
    for ch in chunks:
        b_raw = _mm(vstack(ch["u0"], ch["v"]).T, vstack(ch["b_t"], ch["k_t"]), po)
        ch.update(b_mat=b_raw * bd128 * ch["p_end"])

    states = [state_ref[0], state_ref[1]]
    for step in range(n_chunks):
        todo = [chunks[step], chunks[n_chunks + (n_chunks - 1 - step)]]
        s_in = list(states)
        states = [_mm(s_in[z], ch["a_mat"], SCAN_PRECISE_STATE) + ch["b_mat"] for z, ch in enumerate(todo)]
        wrs = [_mm_nt(vstack(ch["w"], ch["r_t"]), s_in[z], SCAN_PRECISE_STATE) for z, ch in enumerate(todo)]
        for z, ch in enumerate(todo):
            u_full = wrs[z][:c] + ch["u0"]
            y_refs[z][0, rows[ch["j"]], :] = wrs[z][c:] + ch["y0"] + _mm(ch["m_rb"], split_heads(u_full), po)
    state_ref[0] = states[0]
    state_ref[1] = states[1]

    @pl.when(it == n_t - 1)
    def _():
        sf_ref[0, :, 0] = state_ref[...]


def _rwkv_scan(rkvk, ld, a, k_a, s0, tile):
    b, t, _ = rkvk.shape
    n_t = t // tile
    blk = lambda col0, rev: pl.BlockSpec(
        (1, tile, LANES), (lambda bi, hp, i: (bi, n_t - 1 - i, col0 + hp)) if rev else
        (lambda bi, hp, i: (bi, i, col0 + hp)))
    state_spec = pl.BlockSpec((1, 2, 1, LANES, LANES), lambda bi, hp, i: (bi, 0, hp, 0, 0))
    in_specs, args = [], []
    for z in range(2):
        in_specs += [blk(0, z), blk(4, z), blk(8, z), blk(12, z), blk(4 * z, z), blk(4 * z, z)]
        args += [rkvk, rkvk, rkvk, rkvk, ld, a]
    return pl.pallas_call(
        functools.partial(_rwkv_scan_kernel, n_chunks=tile // SCAN_CHUNK),
        grid=(b, 4, n_t),
        in_specs=in_specs + [pl.BlockSpec((1, LANES), lambda bi, hp, i: (0, hp)), state_spec],
        out_specs=[blk(0, 0), blk(0, 1), state_spec],
        out_shape=[
            jax.ShapeDtypeStruct((b, t, D_WIDTH), F32),
            jax.ShapeDtypeStruct((b, t, D_WIDTH), F32),
            jax.ShapeDtypeStruct((b, 2, 4, LANES, LANES), F32),
        ],
        scratch_shapes=[pltpu.VMEM((2, LANES, LANES), F32)],
        compiler_params=_cparams(("parallel", "arbitrary", "arbitrary")),
        name="rwkv_scan",
    )(*args, k_a.reshape(1, D_WIDTH), s0)


def _rwkv_out_kernel(yf_ref, yr_ref, rkvk_ref, a_ref, g_ref, ka_ref, rk_ref, lnw_ref, lnb_ref, bd_ref, o_ref):
    w = D_WIDTH
    bd = bd_ref[...]
    for j in range(w // LANES):
        cs = slice(j * LANES, (j + 1) * LANES)
        y = yf_ref[0, :, cs] + yr_ref[0, :, cs]
        mean = _head_sum(y, bd) * (1.0 / HEAD_DIM)
        yc = y - mean
        var = _head_sum(yc * yc, bd) * (1.0 / HEAD_DIM)
        yn = yc * lax.rsqrt(var + D_GN_EPS) * lnw_ref[:, cs] + lnb_ref[:, cs]
        r = rkvk_ref[0, :, cs]
        k = rkvk_ref[0, :, w + j * LANES:w + (j + 1) * LANES]
        v = rkvk_ref[0, :, 2 * w + j * LANES:2 * w + (j + 1) * LANES]
        bonus = jnp.zeros_like(y)
        for z in range(2):
            a = a_ref[0, :, z * w + j * LANES:z * w + (j + 1) * LANES]
            k_dir = k * (1.0 + (a - 1.0) * ka_ref[:, cs])
            bonus = bonus + _head_sum(r * k_dir * rk_ref[:, cs], bd) * v
        o_ref[0, :, cs] = ((yn + bonus) * g_ref[0, :, cs]).astype(o_ref.dtype)


def _rwkv_out(y_f, y_r, rkvk, a, g, k_a, r_k, ln_w, ln_b, tm):
    b, t, w = y_f.shape
    bd = jnp.asarray(np.kron(np.eye(2), np.ones((HEAD_DIM, HEAD_DIM))), BF16)
    const = lambda bi, i: (0, 0)
    vec = lambda p: p.reshape(1, w).astype(F32)
    return pl.pallas_call(
        _rwkv_out_kernel,
        grid=(b, t // tm),
        in_specs=[
            pl.BlockSpec((1, tm, w), lambda bi, i: (bi, i, 0)),
            pl.BlockSpec((1, tm, w), lambda bi, i: (bi, i, 0)),
            pl.BlockSpec((1, tm, 4 * w), lambda bi, i: (bi, i, 0)),
            pl.BlockSpec((1, tm, 2 * w), lambda bi, i: (bi, i, 0)),
            pl.BlockSpec((1, tm, w), lambda bi, i: (bi, i, 0)),
            pl.BlockSpec((1, w), const),
            pl.BlockSpec((1, w), const),
            pl.BlockSpec((1, w), const),
            pl.BlockSpec((1, w), const),
            pl.BlockSpec((LANES, LANES), const),
        ],
        out_specs=pl.BlockSpec((1, tm, w), lambda bi, i: (bi, i, 0)),
        out_shape=jax.ShapeDtypeStruct((b, t, w), BF16),
        compiler_params=_cparams(("parallel", "arbitrary")),
        name="rwkv_out",
    )(y_f, y_r, rkvk, a, g, vec(k_a), vec(r_k), vec(ln_w), vec(ln_b), bd)


def _mod_vectors(mods_layer, b):
    d = D_MODEL
    lat = [mods_layer[:b, j * d:(j + 1) * d].reshape(b, 1, d) for j in range(6)]
    ctx = [jnp.broadcast_to(mods_layer[b:b + 1, j * d:(j + 1) * d].reshape(1, 1, d), (b, 1, d)) for j in range(6)]
    return lat, ctx


def _tile2(v):
    return jnp.concatenate([v, v]).astype(F32)


def _ab_layer(x_lat, x_ctx, mods_layer, layer, norm_mix, norm_ffn, wg, wu, wd, w_in, w_out,
              a_qn, a_kn, a_lam, a_subln, b_qn, b_kn, b_sink, rope_lat, rope_ctx, need_ctx):
    b, s_len, d = x_lat.shape
    (sh_l, sc_l, g_l, fsh_l, fsc_l, fg_l), (sh_c, sc_c, g_c, fsh_c, fsc_c, fg_c) = _mod_vectors(mods_layer, b)
    lambda_init = 0.8 - 0.6 * math.exp(-0.3 * layer)

    aw = A_HEADS * LANES
    aq, ak, av = w_in[:, :aw], w_in[:, aw:2 * aw], w_in[:, 2 * aw:3 * aw]
    bq = w_in[:, 3 * aw:4 * aw]
    bk = w_in[:, 4 * aw:4 * aw + LANES]
    bv = w_in[:, 4 * aw + LANES:4 * aw + 2 * LANES]
    dup = lambda w: jnp.concatenate([w[:, :64], w[:, :64], w[:, 64:], w[:, 64:]], axis=1)
    w_cat = jnp.concatenate([aq, ak, bq, dup(bk), dup(bv), av], axis=1).astype(BF16)
    scale = HEAD_DIM ** -0.5
    hg = jnp.stack([_tile2(a_qn) * (scale * LOG2E), _tile2(a_kn), _tile2(b_qn) * scale, _tile2(b_kn)]
                   + [jnp.zeros((LANES,), F32)] * 4)
    plan = ([("qk", 0, 0, j) for j in range(4)] + [("qk", 1, 0, 4 + j) for j in range(4)]
            + [("qk", 2, 0, 8 + j) for j in range(4)] + [("qk", 3, 0, 12 + j) for j in range(2)]
            + [("v", 0, 0, 14 + j) for j in range(2)] + [("vT", 0, 1, j) for j in range(4)])
    out_defs = [("rows", 16, BF16), ("vT", 4, BF16)]
    qkv_l, avT_l = _inproj(x_lat, sh_l, sc_l, norm_mix, w_cat, rope_lat[0], rope_lat[1], hg, plan, out_defs, tm=512)
    qkv_c, avT_c = _inproj(x_ctx, sh_c, sc_c, norm_mix, w_cat, rope_ctx[0], rope_ctx[1], hg, plan, out_defs,
                           tm=x_ctx.shape[1])

    lam_f = a_lam.astype(F32)
    lam = (jnp.exp(jnp.sum(lam_f[0] * lam_f[1])) - jnp.exp(jnp.sum(lam_f[2] * lam_f[3])) + lambda_init).reshape(1, 1)
    post = 1.0 - lambda_init
    ak_all = jnp.concatenate([qkv_c[:, :, 4 * LANES:8 * LANES], qkv_l[:, :, 4 * LANES:8 * LANES]], axis=1)
    avT_all = jnp.concatenate([avT_c, avT_l], axis=-1)
    a_lat = _diffattn(lam, qkv_l, 0, ak_all, 0, avT_all, a_subln, post)
    b_lat = _window_attn(b_sink, qkv_l, 8, 12, 14, qkv_c, with_win=True)
    x_lat = _out_ffn(x_lat, a_lat, b_lat, w_out, g_l, norm_ffn, fsh_l, fsc_l, fg_l, wg, wu, wd, tm=256)
    if need_ctx:
        a_ctx = _diffattn(lam, qkv_c, 0, qkv_c, 4, avT_c, a_subln, post)
        b_ctx = _window_attn(b_sink, qkv_c, 8, 12, 14, qkv_c, with_win=False)
        x_ctx = _out_ffn(x_ctx, a_ctx, b_ctx, w_out, g_c, norm_ffn, fsh_c, fsc_c, fg_c, wg, wu, wd, tm=256)
    return x_lat, x_ctx


def _pad_block(m):
    return jnp.concatenate([m, jnp.zeros(m.shape[:-1] + (LANES - m.shape[-1],), m.dtype)], axis=-1)


def _cd_layer(x_lat, x_ctx, mods_layer, norm_mix, norm_ffn, wg, wu, wd, w_in, w_out, c_qn, c_kn, c_rpb,
              d_mu, d_w0, d_w2, d_a0, d_a2, d_g2, d_k_k, d_k_a, d_r_k, d_ln_w, d_ln_b):
    b, s_len, d = x_lat.shape
    n_ctx = x_ctx.shape[1]
    (sh_l, sc_l, g_l, fsh_l, fsc_l, fg_l), (sh_c, sc_c, _, _, _, _) = _mod_vectors(mods_layer, b)
    w = D_WIDTH
    lora = lambda m: [_pad_block(m[..., 3 * w:3 * w + 64]), _pad_block(m[..., 3 * w + 64:3 * w + 128]),
                      _pad_block(m[..., 3 * w + 128:])]
    w_d = w_in[:, 3 * w:]
    w_cat = jnp.concatenate([w_in[:, :3 * w], w_d[:, :3 * w]] + lora(w_d), axis=1).astype(BF16)
    mu_p = jnp.concatenate([d_mu[:3 * w]] + lora(d_mu))
    scale = HEAD_DIM ** -0.5
    hg = jnp.stack([_tile2(c_qn) * scale, _tile2(c_kn)] + [jnp.zeros((LANES,), F32)] * 6)
    plan = ([("qk", 0, 0, j) for j in range(4)] + [("qk", 1, 0, 4 + j) for j in range(4)]
            + [("v", 0, 0, 8 + j) for j in range(4)] + [("raw", 0, 1, j) for j in range(15)])
    out_defs = [("rows", 12, BF16), ("rows", 15, F32)]
    no_rope = lambda n: (jnp.ones((n, LANES), F32), jnp.zeros((n, LANES), F32))
    qkv_l, dcols_l = _inproj(x_lat, sh_l, sc_l, norm_mix, w_cat, *no_rope(s_len), hg, plan, out_defs, tm=512)
    qkv_c, dcols_c = _inproj(x_ctx, sh_c, sc_c, norm_mix, w_cat, *no_rope(n_ctx), hg, plan, out_defs, tm=n_ctx)

    c_lat = _natten(qkv_l, qkv_c, _natten_bias(c_rpb))

    prep = functools.partial(_rwkv_prep, mu=mu_p, w0=d_w0, w2=d_w2, a0=d_a0, a2=d_a2, g2=d_g2, k_k=d_k_k, tm=256)
    rkvk_c, ld_c, a_c, _ = prep(dcols_c)
    rkvk_l, ld_l, a_l, g_l_gate = prep(dcols_l)
    s0 = jnp.zeros((b, 2, 4, LANES, LANES), F32)
    _, _, s_ctx = _rwkv_scan(rkvk_c, ld_c, a_c, d_k_a, s0, tile=min(SCAN_TILE, n_ctx))
    y_f, y_r, _ = _rwkv_scan(rkvk_l, ld_l, a_l, d_k_a, s_ctx, tile=SCAN_TILE)
    d_lat = _rwkv_out(y_f, y_r, rkvk_l, a_l, g_l_gate, d_k_a, d_r_k, d_ln_w, d_ln_b, tm=512)
    return _out_ffn(x_lat, c_lat, d_lat, w_out, g_l, norm_ffn, fsh_l, fsc_l, fg_l, wg, wu, wd, tm=256)


def kernel(x, c, ctx, c_ctx, ada_w, ada_b, norm_mix, norm_ffn, ffn_w_gate, ffn_w_up, ffn_w_down, ab_w_in, ab_w_out, a_q_norm, a_k_norm, a_lambda, a_subln, b_q_norm, b_k_norm, b_sink, cd_w_in, cd_w_out, c_q_norm, c_k_norm, c_rpb, d_mu, d_w0, d_w2, d_a0, d_a2, d_g2, d_k_k, d_k_a, d_r_k, d_ln_w, d_ln_b):
    b, s_len, d = x.shape
    n_ctx = ctx.shape[1]
    cond = jnp.concatenate([c, c_ctx[None, :], jnp.zeros((8 - b - 1, d), F32)], axis=0)
    mods = _adaln(cond, ada_w, ada_b)
    rope_lat = _rope_tables(s_len)
    rope_ctx = (jnp.ones((n_ctx, LANES), F32), jnp.zeros((n_ctx, LANES), F32))
    x_lat, x_ctx = x, ctx
    x_lat, x_ctx = _ab_layer(
        x_lat, x_ctx, mods[0], 0, norm_mix[0], norm_ffn[0], ffn_w_gate[0].astype(BF16), ffn_w_up[0].astype(BF16),
        ffn_w_down[0].astype(BF16), ab_w_in[0], ab_w_out[0].astype(BF16), a_q_norm[0], a_k_norm[0], a_lambda[0],
        a_subln[0], b_q_norm[0], b_k_norm[0], b_sink[0], rope_lat, rope_ctx, need_ctx=True)
    return _cd_layer(
        x_lat, x_ctx, mods[1], norm_mix[1], norm_ffn[1], ffn_w_gate[1].astype(BF16), ffn_w_up[1].astype(BF16),
        ffn_w_down[1].astype(BF16), cd_w_in[0], cd_w_out[0].astype(BF16), c_q_norm[0], c_k_norm[0], c_rpb[0],
        d_mu[0], d_w0[0], d_w2[0], d_a0[0], d_a2[0], d_g2[0], d_k_k[0], d_k_a[0], d_r_k[0], d_ln_w[0], d_ln_b[0])
```

```python
import functools
import math

import jax
import jax.numpy as jnp
import numpy as np
from jax import lax
from jax.experimental import pallas as pl
from jax.experimental.pallas import tpu as pltpu

F32 = jnp.float32
BF16 = jnp.bfloat16

D_MODEL = 1024
GRID_W = 64
HEAD_DIM = 64
ROPE_THETA = 10000.0
NORM_EPS = 1e-6
NEG_INF = -1e30
LANES = 128
A_HEADS = 4
WINDOW = 128
NA_ROWS = 8
NA_COLS = 16
D_HEADS = 8
D_WIDTH = 512
D_GN_EPS = 64e-5
LOG2E = math.log2(math.e)
VMEM_LIMIT = 56 * 1024 * 1024


def _cparams(sem):
    return pltpu.CompilerParams(dimension_semantics=sem, vmem_limit_bytes=VMEM_LIMIT)


def _dot(a, b):
    return jnp.dot(a, b, preferred_element_type=F32)


def _dot_nt(a, b):
    return lax.dot_general(a, b, (((1,), (1,)), ((), ())), preferred_element_type=F32)


def _split(x):
    hi = x.astype(BF16)
    lo = (x - hi.astype(F32)).astype(BF16)
    return hi, lo


def _dot3(a, b):
    ah, al = _split(a)
    bh, bl = _split(b)
    return _dot(ah, bh) + _dot(al, bh) + _dot(ah, bl)


def _dot3_nt(a, b):
    ah, al = _split(a)
    bh, bl = _split(b)
    return _dot_nt(ah, bh) + _dot_nt(al, bh) + _dot_nt(ah, bl)


def _dot2_exact_rhs(a, b_bf16):
    ah, al = _split(a)
    return _dot(ah, b_bf16) + _dot(al, b_bf16)


def _norm_mod(x, gain, shift, scale):
    ms = jnp.mean(x * x, axis=-1, keepdims=True)
    y = x * lax.rsqrt(ms + NORM_EPS) * gain
    return y * (1.0 + scale) + shift


def _adaln_kernel(c_ref, w_ref, b_ref, o_ref):
    c = c_ref[...]
    o_ref[0] = _dot3(jax.nn.silu(c), w_ref[0]) + b_ref[0]


def _adaln(cond, ada_w, ada_b):
    depth, d, n = ada_w.shape
    tn = 1536
    return pl.pallas_call(
        _adaln_kernel,
        grid=(depth, n // tn),
        in_specs=[
            pl.BlockSpec((8, d), lambda l, j: (0, 0)),
            pl.BlockSpec((1, d, tn), lambda l, j: (l, 0, j)),
            pl.BlockSpec((1, 1, tn), lambda l, j: (l, 0, j)),
        ],
        out_specs=pl.BlockSpec((1, 8, tn), lambda l, j: (l, 0, j)),
        out_shape=jax.ShapeDtypeStruct((depth, 8, n), F32),
        compiler_params=_cparams(("arbitrary", "arbitrary")),
        name="adaln",
    )(cond, ada_w, ada_b.reshape(depth, 1, n))


def _head_norm_rope(y, bd, hg, cos, sin):
    ss = _dot2_exact_rhs(y * y, bd)
    y = y * lax.rsqrt(ss * (1.0 / HEAD_DIM) + NORM_EPS) * hg
    lane = lax.broadcasted_iota(jnp.int32, y.shape, 1)
    first_half = (lane % 32) < 16
    partner = jnp.where(first_half, pltpu.roll(y, LANES - 16, axis=1), pltpu.roll(y, 16, axis=1))
    return y * cos + partner * sin


def _inproj_kernel(x_ref, sh_ref, sc_ref, gain_ref, w_ref, cos_ref, sin_ref, hg_ref, bd_ref, *out_refs, plan):
    h = _norm_mod(x_ref[0], gain_ref[...], sh_ref[0], sc_ref[0]).astype(BF16)
    bd = bd_ref[...]
    cos = cos_ref[...]
    sin = sin_ref[...]
    n_blocks = len(plan)
    for j0 in range(0, n_blocks, 2):
        width = min(2, n_blocks - j0) * LANES
        y2 = _dot(h, w_ref[:, j0 * LANES:j0 * LANES + width])
        for jj in range(width // LANES):
            kind, hg_idx, out_idx, out_blk = plan[j0 + jj]
            y = y2[:, jj * LANES:(jj + 1) * LANES]
            o_ref = out_refs[out_idx]
            if kind == "qk":
                y = _head_norm_rope(y, bd, hg_ref[hg_idx:hg_idx + 1, :], cos, sin)
                o_ref[0, :, out_blk * LANES:(out_blk + 1) * LANES] = y.astype(o_ref.dtype)
            elif kind == "vT":
                o_ref[0, out_blk] = y.T.astype(o_ref.dtype)
            else:
                o_ref[0, :, out_blk * LANES:(out_blk + 1) * LANES] = y.astype(o_ref.dtype)


def _inproj(x, shift, scale, gain, w, cos, sin, hg, plan, out_defs, tm):
    b, t, d = x.shape
    n = w.shape[1]
    bd = jnp.asarray(np.kron(np.eye(2), np.ones((HEAD_DIM, HEAD_DIM))), BF16)
    out_shapes, out_specs = [], []
    for kind, nblk, dt in out_defs:
        if kind == "rows":
            out_shapes.append(jax.ShapeDtypeStruct((b, t, nblk * LANES), dt))
            out_specs.append(pl.BlockSpec((1, tm, nblk * LANES), lambda bi, i: (bi, i, 0)))
        else:
            out_shapes.append(jax.ShapeDtypeStruct((b, nblk, LANES, t), dt))
            out_specs.append(pl.BlockSpec((1, nblk, LANES, tm), lambda bi, i: (bi, 0, 0, i)))
    return pl.pallas_call(
        functools.partial(_inproj_kernel, plan=tuple(plan)),
        grid=(b, t // tm),
        in_specs=[
            pl.BlockSpec((1, tm, d), lambda bi, i: (bi, i, 0)),
            pl.BlockSpec((1, 1, d), lambda bi, i: (bi, 0, 0)),
            pl.BlockSpec((1, 1, d), lambda bi, i: (bi, 0, 0)),
            pl.BlockSpec((1, d), lambda bi, i: (0, 0)),
            pl.BlockSpec((d, n), lambda bi, i: (0, 0)),
            pl.BlockSpec((tm, LANES), lambda bi, i: (i, 0)),
            pl.BlockSpec((tm, LANES), lambda bi, i: (i, 0)),
            pl.BlockSpec(hg.shape, lambda bi, i: (0, 0)),
            pl.BlockSpec((LANES, LANES), lambda bi, i: (0, 0)),
        ],
        out_specs=out_specs,
        out_shape=out_shapes,
        compiler_params=_cparams(("parallel", "arbitrary")),
        name="inproj",
    )(x, shift, scale, gain.reshape(1, d), w, cos, sin, hg, bd)


def _rope_tables(n_tokens):
    axis_dim = HEAD_DIM // 2
    freqs = ROPE_THETA ** (-jnp.arange(0, axis_dim, 2, dtype=F32) / axis_dim)
    t = jnp.arange(n_tokens, dtype=jnp.int32)
    lane = np.arange(LANES)
    d = lane % HEAD_DIM
    use_col = (d // 32) == 1
    f_idx = (d % 32) % 16
    sign = np.where((d % 32) < 16, -1.0, 1.0).astype(np.float32)
    pos = jnp.where(use_col[None, :], (t % GRID_W)[:, None], (t // GRID_W)[:, None]).astype(F32)
    ang = pos * freqs[f_idx][None, :]
    return jnp.cos(ang), jnp.sin(ang) * sign[None, :]


def _diffattn_kernel(lam_ref, q_ref, k_ref, vT_ref, subln_ref, o_ref, *s_refs, tk, n_tiles, post_scale):
    q = q_ref[0]
    tq = q.shape[0]
    lane = lax.broadcasted_iota(jnp.int32, q.shape, 1)
    zero = jnp.zeros_like(q)
    qq = jnp.concatenate([jnp.where(lane < HEAD_DIM, q, zero), jnp.where(lane >= HEAD_DIM, q, zero)], axis=0)
    qq_t = qq.astype(F32).T.astype(BF16)

    sub = DIFF_SUB_BLOCK
    n_sub = tk // sub

    def scores(i, s_ref):
        s = _dot(k_ref[0, i * tk:(i + 1) * tk, :], qq_t)
        s_ref[...] = s
        return jnp.max(s, axis=0, keepdims=True)

    m = jnp.full((1, 2 * tq), -jnp.inf, F32)
    l = jnp.zeros((1, 2 * tq), F32)
    acc = jnp.zeros((LANES, 2 * tq), F32)
    ahead = len(s_refs) - 1
    cmaxes = {}
    for i in range(min(ahead, n_tiles)):
        cmaxes[i] = scores(i, s_refs[i % len(s_refs)])
    for i in range(n_tiles):
        m_new = jnp.maximum(m, cmaxes.pop(i))
        alpha = jnp.exp2(m - m_new)
        s_ref = s_refs[i % len(s_refs)]
        pv, psum = None, None
        if i + ahead < n_tiles:
            cmaxes[i + ahead] = scores(i + ahead, s_refs[(i + ahead) % len(s_refs)])
        for j in range(n_sub):
            p = jnp.exp2(s_ref[j * sub:(j + 1) * sub, :] - m_new)
            ps = jnp.sum(p, axis=0, keepdims=True)
            d = _dot(vT_ref[0, 0, :, i * tk + j * sub:i * tk + (j + 1) * sub], p.astype(BF16))
            psum = ps if psum is None else psum + ps
            pv = d if pv is None else pv + d
        l = alpha * l + psum
        acc = alpha * acc + pv
        m = m_new
    o = acc / l
    oT = o[:, :tq] - lam_ref[0, 0] * o[:, tq:]
    ms = jnp.mean(oT * oT, axis=0, keepdims=True)
    oT = oT * lax.rsqrt(ms + NORM_EPS) * subln_ref[...] * post_scale
    o_ref[0] = oT.T.astype(o_ref.dtype)


DIFF_KEY_TILE = 1280
DIFF_SUB_BLOCK = 256
DIFF_LOOKAHEAD = 2


def _diffattn(lam, q_arr, q_blk0, k_arr, k_blk0, vT, subln, post_scale, tq=128):
    b, t_q, _ = q_arr.shape
    t_k = k_arr.shape[1]
    tk = DIFF_KEY_TILE if (t_k % DIFF_KEY_TILE == 0 and t_k > DIFF_KEY_TILE) else 256
    return pl.pallas_call(
        functools.partial(_diffattn_kernel, tk=tk, n_tiles=t_k // tk, post_scale=post_scale),
        grid=(b, A_HEADS, t_q // tq),
        in_specs=[
            pl.BlockSpec(memory_space=pltpu.SMEM),
            pl.BlockSpec((1, tq, LANES), lambda bi, h, i: (bi, i, q_blk0 + h)),
            pl.BlockSpec((1, t_k, LANES), lambda bi, h, i: (bi, 0, k_blk0 + h)),
            pl.BlockSpec((1, 1, LANES, t_k), lambda bi, h, i: (bi, h, 0, 0)),
            pl.BlockSpec((LANES, 1), lambda bi, h, i: (0, 0)),
        ],
        out_specs=pl.BlockSpec((1, tq, LANES), lambda bi, h, i: (bi, i, h)),
        out_shape=jax.ShapeDtypeStruct((b, t_q, A_HEADS * LANES), BF16),
        scratch_shapes=[pltpu.VMEM((tk, 2 * tq), F32)] * (DIFF_LOOKAHEAD + 1),
        compiler_params=_cparams(("parallel", "arbitrary", "arbitrary")),
        name="diffattn",
    )(lam, q_arr, k_arr, vT, subln.reshape(LANES, 1))


def _window_kernel(sink_ref, q_ref, kc_ref, vc_ref, *rest, with_win, n_blocks):
    if with_win:
        kp_ref, k0_ref, kn_ref, vp_ref, v0_ref, vn_ref, o_ref = rest
    else:
        (o_ref,) = rest
    i = pl.program_id(1)
    tq = q_ref.shape[1]
    n_ctx = kc_ref.shape[1]
    lane = lax.broadcasted_iota(jnp.int32, (1, LANES), 1)
    lo = lane < HEAD_DIM
    if with_win:
        r = lax.broadcasted_iota(jnp.int32, (tq, tq), 0)
        c = lax.broadcasted_iota(jnp.int32, (tq, tq), 1)
        ok_prev = jnp.logical_and(c >= r, i > 0)
        ok_next = jnp.logical_and(c <= r, i < n_blocks - 1)
        valid = jnp.concatenate([ok_prev, jnp.ones((tq, tq), jnp.bool_), ok_next,
                                 jnp.ones((tq, n_ctx), jnp.bool_)], axis=1)
    scores, vcats = [], []
    for g in range(2):
        gs = slice(g * LANES, (g + 1) * LANES)
        if with_win:
            kcat = jnp.concatenate([kp_ref[0, :, gs], k0_ref[0, :, gs], kn_ref[0, :, gs], kc_ref[0, :, gs]], axis=0)
            vcat = jnp.concatenate([vp_ref[0, :, gs], v0_ref[0, :, gs], vn_ref[0, :, gs], vc_ref[0, :, gs]], axis=0)
        else:
            kcat = kc_ref[0, :, gs]
            vcat = vc_ref[0, :, gs]
        q_rows = []
        for cb in range(2):
            qb = q_ref[0, :, (g * 2 + cb) * LANES:(g * 2 + cb + 1) * LANES]
            zq = jnp.zeros_like(qb)
            q_rows += [jnp.where(lo, qb, zq), jnp.where(lo, zq, qb)]
        scores.append(_dot_nt(jnp.concatenate(q_rows, axis=0), kcat))
        vcats.append(vcat)
    for g in range(2):
        s = scores[g]
        if with_win:
            s = jnp.where(jnp.concatenate([valid] * 4, axis=0), s, NEG_INF)
        snk = jnp.concatenate([jnp.broadcast_to(sink_ref[g * 4 + h:g * 4 + h + 1, 0:1], (tq, 1)) for h in range(4)],
                              axis=0)
        m = jnp.maximum(jnp.max(s, axis=-1, keepdims=True), snk)
        e = jnp.exp(s - m)
        denom = jnp.sum(e, axis=-1, keepdims=True) + jnp.exp(snk - m)
        o = _dot(e.astype(BF16), vcats[g]) / denom
        for cb in range(2):
            out = jnp.where(lo, o[(2 * cb) * tq:(2 * cb + 1) * tq], o[(2 * cb + 1) * tq:(2 * cb + 2) * tq])
            o_ref[0, :, (g * 2 + cb) * LANES:(g * 2 + cb + 1) * LANES] = out.astype(o_ref.dtype)


def _window_attn(sink, q_arr, q_blk0, k_blk0, v_blk0, ctx_arr, with_win):
    b, t_q, _ = q_arr.shape
    n_ctx = ctx_arr.shape[1]
    tq = WINDOW
    nb = t_q // tq
    sink_tab = jnp.broadcast_to(sink.astype(F32).reshape(8, 1), (8, LANES))
    in_specs = [
        pl.BlockSpec((8, LANES), lambda bi, i: (0, 0)),
        pl.BlockSpec((1, tq, 4 * LANES), lambda bi, i: (bi, i, q_blk0 // 4)),
        pl.BlockSpec((1, n_ctx, 2 * LANES), lambda bi, i: (bi, 0, k_blk0 // 2)),
        pl.BlockSpec((1, n_ctx, 2 * LANES), lambda bi, i: (bi, 0, v_blk0 // 2)),
    ]
    args = [sink_tab, q_arr, ctx_arr, ctx_arr]
    if with_win:
        for blk0 in (k_blk0, v_blk0):
            in_specs += [
                pl.BlockSpec((1, tq, 2 * LANES), lambda bi, i, c=blk0 // 2: (bi, jnp.maximum(i - 1, 0), c)),
                pl.BlockSpec((1, tq, 2 * LANES), lambda bi, i, c=blk0 // 2: (bi, i, c)),
                pl.BlockSpec((1, tq, 2 * LANES), lambda bi, i, c=blk0 // 2: (bi, jnp.minimum(i + 1, nb - 1), c)),
            ]
            args += [q_arr, q_arr, q_arr]
    return pl.pallas_call(
        functools.partial(_window_kernel, with_win=with_win, n_blocks=nb),
        grid=(b, nb),
        in_specs=in_specs,
        out_specs=pl.BlockSpec((1, tq, 4 * LANES), lambda bi, i: (bi, i, 0)),
        out_shape=jax.ShapeDtypeStruct((b, t_q, 4 * LANES), BF16),
        compiler_params=_cparams(("parallel", "arbitrary")),
        name="window_attn",
    )(*args)


FFN_ROW_TILE = 512


def _out_ffn_kernel(x_ref, a_ref, b_ref, woa_ref, wob_ref, gate_ref, ng_ref, fsh_ref, fsc_ref, fg_ref,
                    wg_ref, wu_ref, wd_ref, o_ref):
    o = _dot(a_ref[0], woa_ref[...]) + _dot(b_ref[0], wob_ref[...])
    x1 = x_ref[0] + gate_ref[0] * o
    h = _norm_mod(x1, ng_ref[...], fsh_ref[0], fsc_ref[0]).astype(BF16)
    act = (jax.nn.silu(_dot(h, wg_ref[...])) * _dot(h, wu_ref[...])).astype(BF16)
    o_ref[0] = x1 + fg_ref[0] * _dot(act, wd_ref[...])


def _out_ffn(x, mix_a, mix_b, wo, gate, ng, fsh, fsc, fg, wg, wu, wd, tm):
    b, t, d = x.shape
    na, nb_ = mix_a.shape[-1], mix_b.shape[-1]
    dff = wg.shape[1]
    row = lambda bi, i: (bi, i, 0)
    vec = lambda bi, i: (bi, 0, 0)
    const = lambda bi, i: (0, 0)
    resident = functools.partial(pl.BlockSpec, index_map=const, pipeline_mode=pl.Buffered(1))
    return pl.pallas_call(
        _out_ffn_kernel,
        grid=(b, t // tm),
        in_specs=[
            pl.BlockSpec((1, tm, d), row),
            pl.BlockSpec((1, tm, na), row),
            pl.BlockSpec((1, tm, nb_), row),
            resident((na, d)),
            resident((nb_, d)),
            pl.BlockSpec((1, 1, d), vec),
            resident((1, d)),
            pl.BlockSpec((1, 1, d), vec),
            pl.BlockSpec((1, 1, d), vec),
            pl.BlockSpec((1, 1, d), vec),
            resident((d, dff)),
            resident((d, dff)),
            resident((dff, d)),
        ],
        out_specs=pl.BlockSpec((1, tm, d), row),
        out_shape=jax.ShapeDtypeStruct((b, t, d), F32),
        compiler_params=_cparams(("parallel", "arbitrary")),
        name="out_ffn",
    )(x, mix_a, mix_b, wo[:na], wo[na:], gate, ng.reshape(1, d), fsh, fsc, fg, wg, wu, wd)


NA_ROWS_PER_STEP = 8


def _natten_kernel(q_ref, k_ref, v_ref, kc_ref, vc_ref, bias_ref, o_ref, *, n_rows):
    step = pl.program_id(2)
    n_win = NA_ROWS * GRID_W
    lane = lax.broadcasted_iota(jnp.int32, (1, LANES), 1)
    lo = lane < HEAD_DIM
    keeps = (lo, jnp.logical_not(lo))
    q_all = q_ref[0]
    zq = jnp.zeros_like(q_all)
    q_half = [jnp.where(keep, q_all, zq) for keep in keeps]
    s_ctx = [_dot_nt(qh, kc_ref[0]) for qh in q_half]
    row_info, s_win = [], []
    for rr in range(NA_ROWS_PER_STEP):
        i = step * NA_ROWS_PER_STEP + rr
        r0 = jnp.clip(i - NA_ROWS // 2, 0, n_rows - NA_ROWS)
        koff = pl.multiple_of(r0 * GRID_W, GRID_W)
        row_info.append((i - r0, koff))
        k_win = k_ref[0, pl.ds(koff, n_win), :]
        for half in range(2):
            s_win.append(_dot_nt(q_half[half][rr * GRID_W:(rr + 1) * GRID_W], k_win))
    e_win, e_ctx, denom = [], [[], []], [[], []]
    for rr in range(NA_ROWS_PER_STEP):
        qs = slice(rr * GRID_W, (rr + 1) * GRID_W)
        for half in range(2):
            sw = s_win[2 * rr + half] + bias_ref[row_info[rr][0], half]
            sc = s_ctx[half][qs]
            m = jnp.maximum(jnp.max(sw, axis=-1, keepdims=True), jnp.max(sc, axis=-1, keepdims=True))
            ew, ec = jnp.exp(sw - m), jnp.exp(sc - m)
            e_win.append(ew.astype(BF16))
            e_ctx[half].append(ec.astype(BF16))
            denom[half].append(jnp.sum(ew, axis=-1, keepdims=True) + jnp.sum(ec, axis=-1, keepdims=True))
    vc = vc_ref[0]
    zc = jnp.zeros_like(vc)
    o_ctx = [_dot(jnp.concatenate(e_ctx[half], axis=0), jnp.where(keeps[half], vc, zc)) for half in range(2)]
    for rr in range(NA_ROWS_PER_STEP):
        qs = slice(rr * GRID_W, (rr + 1) * GRID_W)
        v_win = v_ref[0, pl.ds(row_info[rr][1], n_win), :]
        zv = jnp.zeros_like(v_win)
        out = None
        for half in range(2):
            o = (_dot(e_win[2 * rr + half], jnp.where(keeps[half], v_win, zv)) + o_ctx[half][qs]) / denom[half][rr]
            out = o if out is None else out + o
        o_ref[0, qs, :] = out.astype(o_ref.dtype)


def _natten_bias(rpb):
    n_heads = rpb.shape[0]
    qj = np.arange(GRID_W)[:, None]
    kj = np.arange(GRID_W)[None, :]
    cstart = np.clip(qj - NA_COLS // 2, 0, GRID_W - NA_COLS)
    in_win = (kj >= cstart) & (kj < cstart + NA_COLS)
    pad = GRID_W - NA_COLS
    rpb_p = jnp.pad(rpb.astype(F32), ((0, 0), (0, 0), (pad, pad)))
    toep = jnp.stack([rpb_p[:, :, pad + NA_COLS - 1 - q:pad + NA_COLS - 1 - q + GRID_W] for q in range(GRID_W)],
                     axis=2)
    toep = jnp.where(in_win[None, None], toep, NEG_INF)
    per_class = [jnp.moveaxis(toep[:, NA_ROWS - 1 - d:2 * NA_ROWS - 1 - d], 1, 2) for d in range(NA_ROWS)]
    return jnp.stack(per_class).reshape(NA_ROWS, n_heads, GRID_W, NA_ROWS * GRID_W)


def _natten(qkv, qkv_ctx, bias):
    b, s_len, _ = qkv.shape
    n_ctx = qkv_ctx.shape[1]
    n_rows = s_len // GRID_W
    tq = NA_ROWS_PER_STEP * GRID_W
    return pl.pallas_call(
        functools.partial(_natten_kernel, n_rows=n_rows),
        grid=(b, 4, n_rows // NA_ROWS_PER_STEP),
        in_specs=[
            pl.BlockSpec((1, tq, LANES), lambda bi, hp, i: (bi, i, hp)),
            pl.BlockSpec((1, s_len, LANES), lambda bi, hp, i: (bi, 0, 4 + hp)),
            pl.BlockSpec((1, s_len, LANES), lambda bi, hp, i: (bi, 0, 8 + hp)),
            pl.BlockSpec((1, n_ctx, LANES), lambda bi, hp, i: (bi, 0, 4 + hp)),
            pl.BlockSpec((1, n_ctx, LANES), lambda bi, hp, i: (bi, 0, 8 + hp)),
            pl.BlockSpec((NA_ROWS, 2, GRID_W, NA_ROWS * GRID_W), lambda bi, hp, i: (0, hp, 0, 0)),
        ],
        out_specs=pl.BlockSpec((1, tq, LANES), lambda bi, hp, i: (bi, i, hp)),
        out_shape=jax.ShapeDtypeStruct((b, s_len, 4 * LANES), BF16),
        compiler_params=_cparams(("parallel", "arbitrary", "arbitrary")),
        name="natten",
    )(qkv, qkv, qkv, qkv_ctx, qkv_ctx, bias)


def _head_sum(x, bd):
    return _dot2_exact_rhs(x, bd)


def _rwkv_prep_kernel(x_ref, xp_ref, xn_ref, mu_ref, w0_ref, w2_ref, a0_ref, a2_ref, g2_ref, kk_ref, bd_ref,
                      rkvk_ref, ld_ref, a_ref, g_ref):
    i = pl.program_id(1)
    n_t = pl.num_programs(1)
    x = x_ref[0]
    tm = x.shape[0]
    prev_row = jnp.where(i > 0, xp_ref[0, 7:8, :], 0.0)
    next_row = jnp.where(i < n_t - 1, xn_ref[0, 0:1, :], 0.0)
    rows = lax.broadcasted_iota(jnp.int32, (tm, 1), 0)
    x_prev = jnp.where(rows == 0, prev_row, pltpu.roll(x, 1, axis=0))
    x_next = jnp.where(rows == tm - 1, next_row, pltpu.roll(x, tm - 1, axis=0))
    xs = x + mu_ref[...] * (0.5 * (x_prev + x_next) - x)
    w = D_WIDTH
    k = xs[:, w:2 * w]
    rkvk_ref[0, :, 0:3 * w] = xs[:, 0:3 * w]
    kk = k * kk_ref[...]
    nrm = jnp.sqrt(jnp.concatenate([_head_sum(kk[:, j * LANES:(j + 1) * LANES] ** 2, bd_ref[...])
                                    for j in range(w // LANES)], axis=1))
    rkvk_ref[0, :, 3 * w:4 * w] = kk / jnp.maximum(nrm, 1e-12)
    wd = xs[:, 3 * w:3 * w + LANES]
    ad = xs[:, 3 * w + LANES:3 * w + 2 * LANES]
    gd = xs[:, 3 * w + 2 * LANES:3 * w + 3 * LANES]
    wraw = -jax.nn.softplus(-(w0_ref[...] + _dot3(jnp.tanh(wd), w2_ref[...]))) - 0.5
    ld_ref[0] = -jnp.exp(wraw)
    a_ref[0] = jax.nn.sigmoid(a0_ref[...] + _dot3(ad, a2_ref[...]))
    g_ref[0] = _dot3(jax.nn.sigmoid(gd), g2_ref[...])


def _rwkv_prep(dcols, mu, w0, w2, a0, a2, g2, k_k, tm):
    b, t, n = dcols.shape
    w = D_WIDTH
    bd = jnp.asarray(np.kron(np.eye(2), np.ones((HEAD_DIM, HEAD_DIM))), BF16)
    pad_rows = lambda m, r0: jnp.zeros((LANES, m.shape[1]), F32).at[r0:r0 + m.shape[0]].set(m)
    w2p = jnp.concatenate([pad_rows(w2[0], 0), pad_rows(w2[1], 32)], axis=1)
    a2p = jnp.concatenate([pad_rows(a2[0], 0), pad_rows(a2[1], 32)], axis=1)
    g2p = pad_rows(g2, 0)
    const = lambda bi, i: (0, 0)
    t8 = tm // 8
    return pl.pallas_call(
        _rwkv_prep_kernel,
        grid=(b, t // tm),
        in_specs=[
            pl.BlockSpec((1, tm, n), lambda bi, i: (bi, i, 0)),
            pl.BlockSpec((1, 8, n), lambda bi, i: (bi, jnp.maximum(i * t8 - 1, 0), 0)),
            pl.BlockSpec((1, 8, n), lambda bi, i: (bi, jnp.minimum((i + 1) * t8, t // 8 - 1), 0)),
            pl.BlockSpec((1, n), const),
            pl.BlockSpec((1, 2 * w), const),
            pl.BlockSpec((LANES, 2 * w), const),
            pl.BlockSpec((1, 2 * w), const),
            pl.BlockSpec((LANES, 2 * w), const),
            pl.BlockSpec((LANES, w), const),
            pl.BlockSpec((1, w), const),
            pl.BlockSpec((LANES, LANES), const),
        ],
        out_specs=[
            pl.BlockSpec((1, tm, 4 * w), lambda bi, i: (bi, i, 0)),
            pl.BlockSpec((1, tm, 2 * w), lambda bi, i: (bi, i, 0)),
            pl.BlockSpec((1, tm, 2 * w), lambda bi, i: (bi, i, 0)),
            pl.BlockSpec((1, tm, w), lambda bi, i: (bi, i, 0)),
        ],
        out_shape=[
            jax.ShapeDtypeStruct((b, t, 4 * w), F32),
            jax.ShapeDtypeStruct((b, t, 2 * w), F32),
            jax.ShapeDtypeStruct((b, t, 2 * w), F32),
            jax.ShapeDtypeStruct((b, t, w), F32),
        ],
        compiler_params=_cparams(("parallel", "arbitrary")),
        name="rwkv_prep",
    )(dcols, dcols, dcols, mu.reshape(1, n), w0.reshape(1, 2 * w), w2p, a0.reshape(1, 2 * w), a2p, g2p,
      k_k.reshape(1, w), bd)


SCAN_CHUNK = 64
SCAN_TILE = 512


def _split3(x):
    hi = x.astype(BF16)
    r1 = x - hi.astype(F32)
    mid = r1.astype(BF16)
    lo = (r1 - mid.astype(F32)).astype(BF16)
    return hi, mid, lo


def _mm(a, b, precise):
    return _dot3(a, b) if precise else _dot(a.astype(BF16), b.astype(BF16))


def _mm_nt(a, b, precise):
    return _dot3_nt(a, b) if precise else _dot_nt(a.astype(BF16), b.astype(BF16))


def _blockwise_mm(x, y, bd_mask, precise):
    n_blk = x.shape[1] // SCAN_CHUNK
    expand = lambda m: jnp.where(bd_mask, jnp.concatenate([m] * n_blk, axis=0), jnp.zeros((), m.dtype))
    if not precise:
        return _dot(x.astype(BF16), expand(y.astype(BF16)))
    xh, xl = _split(x)
    yh, yl = _split(y)
    ybd = expand(yh)
    return _dot(xh, ybd) + _dot(xl, ybd) + _dot(xh, expand(yl))


SCAN_PRECISE_INVERSE = False
SCAN_PRECISE_STATE = False
SCAN_PRECISE_OTHER = False


def _rwkv_scan_kernel(*refs, n_chunks):
    fwd_refs, rev_refs = refs[0:6], refs[6:12]
    ka_ref, s0_ref = refs[12:14]
    y_refs = refs[14:16]
    sf_ref, state_ref = refs[16:18]
    it = pl.program_id(2)
    n_t = pl.num_programs(2)
    c = SCAN_CHUNK
    n_units = n_chunks // 2

    @pl.when(it == 0)
    def _():
        state_ref[...] = s0_ref[0, :, 0]

    ri = lax.broadcasted_iota(jnp.int32, (c, c), 0)
    ci = lax.broadcasted_iota(jnp.int32, (c, c), 1)
    eye = (ri == ci).astype(F32)
    tile_l = lambda m, n: jnp.concatenate([m] * n, axis=1)
    eye4 = tile_l(eye, 4)
    blk = lambda n: ((ri // n) == (ci // n)).astype(F32)
    m8, m16, m32 = tile_l(blk(8), 4), tile_l(blk(16), 4), tile_l(blk(32), 4)
    r4 = lax.broadcasted_iota(jnp.int32, (4 * c, 4 * c), 0)
    c4 = lax.broadcasted_iota(jnp.int32, (4 * c, 4 * c), 1)
    bd4 = (r4 // c) == (c4 // c)
    lane = lax.broadcasted_iota(jnp.int32, (1, LANES), 1)
    hm_lo = (lane < HEAD_DIM).astype(F32)
    hm_hi = 1.0 - hm_lo
    r2 = lax.broadcasted_iota(jnp.int32, (LANES, LANES), 0)
    c2 = lax.broadcasted_iota(jnp.int32, (LANES, LANES), 1)
    bd128 = ((r2 // HEAD_DIM) == (c2 // HEAD_DIM)).astype(F32)
    eye128 = (r2 == c2).astype(F32)
    ka = ka_ref[...]
    vstack = lambda *xs: jnp.concatenate(xs, axis=0)
    hstack = lambda *xs: jnp.concatenate(xs, axis=1)
    split_heads = lambda m: vstack(m * hm_lo, m * hm_hi)
    po = SCAN_PRECISE_OTHER

    rows = [slice(j * c, (j + 1) * c) for j in range(n_chunks)]
    masks = []
    for z in range(2):
        strict = ((ci > ri) if z == 1 else (ci < ri)).astype(F32)
        incl = strict + eye
        masks.append(dict(strict2=tile_l(strict, 2), incl2=tile_l(incl, 2), tri_b=incl.astype(BF16)))

    cums = []
    for z, in_refs in enumerate((fwd_refs, rev_refs)):
        h3 = _split3(hstack(*[in_refs[4][0, rw, :] for rw in rows]))
        tri_b = masks[z]["tri_b"]
        cums.append(_dot(tri_b, h3[0]) + _dot(tri_b, h3[1]) + _dot(tri_b, h3[2]))

    chunks = []
    for z, in_refs in enumerate((fwd_refs, rev_refs)):
        r_ref, k_ref, v_ref, kk_ref, ld_ref, a_ref = in_refs
        for j, rw in enumerate(rows):
            r, k, v, kk = r_ref[0, rw, :], k_ref[0, rw, :], v_ref[0, rw, :], kk_ref[0, rw, :]
            ld, a = ld_ref[0, rw, :], a_ref[0, rw, :]
            cum = cums[z][:, j * LANES:(j + 1) * LANES]
            pinv = jnp.exp(-cum)
            p_end = jnp.exp(cum[0:1, :] if z == 1 else cum[c - 1:c, :])
            r_t = r * jnp.exp(cum)
            a_t = -kk * jnp.exp(cum - ld)
            b_t = kk * a * pinv
            k_t = k * (1.0 + (a - 1.0) * ka) * pinv
            chunks.append(dict(z=z, j=j, r_t=r_t, a_t=a_t, b_t=b_t, k_t=k_t, v=v, p_end=p_end))
    for ch in chunks:
        g = _mm_nt(vstack(ch["a_t"], ch["r_t"]), vstack(split_heads(ch["b_t"]), split_heads(ch["k_t"])), po)
        mk = masks[ch["z"]]
        ch.update(l_ab=g[:c, :2 * c] * mk["strict2"], l_ak=g[:c, 2 * c:] * mk["strict2"],
                  m_rb=g[c:, :2 * c] * mk["incl2"], m_rk=g[c:, 2 * c:] * mk["incl2"])
    for ch in chunks:
        lv_y0 = _mm(vstack(ch["l_ak"], ch["m_rk"]), split_heads(ch["v"]), po)
        ch.update(lv=lv_y0[:c], y0=lv_y0[c:])

    mm4 = lambda x, y: _blockwise_mm(x, y, bd4, SCAN_PRECISE_INVERSE)
    l4s = [hstack(chunks[2 * u]["l_ab"], chunks[2 * u + 1]["l_ab"]) for u in range(2 * n_units)]
    p1s = [l4 * m8 for l4 in l4s]
    p2s = [mm4(p1, p1) for p1 in p1s]
    xs = [eye4 + p1 for p1 in p1s]
    xs = [x + mm4(x, p2) for x, p2 in zip(xs, p2s)]
    p4s = [mm4(p2, p2) for p2 in p2s]
    xs = [x + mm4(x, p4) for x, p4 in zip(xs, p4s)]
    for inner, outer in ((m8, m16), (m16, m32), (m32, None)):
        sel = (1.0 - inner) if outer is None else (outer - inner)
        ys = [mm4(x, l4 * sel) for x, l4 in zip(xs, l4s)]
        xs = [x + mm4(y, x) for x, y in zip(xs, ys)]

    for i, ch in enumerate(chunks):
        t_inv = xs[i // 2][:, (i % 2) * LANES:(i % 2 + 1) * LANES]
        w_u0 = _mm(t_inv, hstack(split_heads(ch["a_t"]), split_heads(ch["lv"])), po)
        ch.update(w=w_u0[:, :LANES], u0=w_u0[:, LANES:])
    for ch in chunks:
        m = _mm(ch["w"].T, ch["b_t"], po)
        ch.update(a_mat=(eye128 + m * bd128) * ch["p_end"])
    for ch in chunks:
        b_raw = _mm(vstack(ch["u0"], ch["v"]).T, vstack(ch["b_t"], ch["k_t"]), po)
        ch.update(b_mat=b_raw * bd128 * ch["p_end"])

    states = [state_ref[0], state_ref[1]]
    for step in range(n_chunks):
        todo = [chunks[step], chunks[n_chunks + (n_chunks - 1 - step)]]
        s_in = list(states)
        states = [_mm(s_in[z], ch["a_mat"], SCAN_PRECISE_STATE) + ch["b_mat"] for z, ch in enumerate(todo)]
        wrs = [_mm_nt(vstack(ch["w"], ch["r_t"]), s_in[z], SCAN_PRECISE_STATE) for z, ch in enumerate(todo)]
        for z, ch in enumerate(todo):
            u_full = wrs[z][:c] + ch["u0"]
            y_refs[z][0, rows[ch["j"]], :] = wrs[z][c:] + ch["y0"] + _mm(ch["m_rb"], split_heads(u_full), po)
    state_ref[0] = states[0]
    state_ref[1] = states[1]

    @pl.when(it == n_t - 1)
    def _():
        sf_ref[0, :, 0] = state_ref[...]


def _rwkv_scan(rkvk, ld, a, k_a, s0, tile):
    b, t, _ = rkvk.shape
    n_t = t // tile
    blk = lambda col0, rev: pl.BlockSpec(
        (1, tile, LANES), (lambda bi, hp, i: (bi, n_t - 1 - i, col0 + hp)) if rev else
        (lambda bi, hp, i: (bi, i, col0 + hp)))
    state_spec = pl.BlockSpec((1, 2, 1, LANES, LANES), lambda bi, hp, i: (bi, 0, hp, 0, 0))
    in_specs, args = [], []
    for z in range(2):
        in_specs += [blk(0, z), blk(4, z), blk(8, z), blk(12, z), blk(4 * z, z), blk(4 * z, z)]
        args += [rkvk, rkvk, rkvk, rkvk, ld, a]
    return pl.pallas_call(
        functools.partial(_rwkv_scan_kernel, n_chunks=tile // SCAN_CHUNK),
        grid=(b, 4, n_t),
        in_specs=in_specs + [pl.BlockSpec((1, LANES), lambda bi, hp, i: (0, hp)), state_spec],
        out_specs=[blk(0, 0), blk(0, 1), state_spec],
        out_shape=[
            jax.ShapeDtypeStruct((b, t, D_WIDTH), F32),
            jax.ShapeDtypeStruct((b, t, D_WIDTH), F32),
            jax.ShapeDtypeStruct((b, 2, 4, LANES, LANES), F32),
        ],
        scratch_shapes=[pltpu.VMEM((2, LANES, LANES), F32)],
        compiler_params=_cparams(("parallel", "arbitrary", "arbitrary")),
        name="rwkv_scan",
    )(*args, k_a.reshape(1, D_WIDTH), s0)


def _rwkv_out_kernel(yf_ref, yr_ref, rkvk_ref, a_ref, g_ref, ka_ref, rk_ref, lnw_ref, lnb_ref, bd_ref, o_ref):
    w = D_WIDTH
    bd = bd_ref[...]
    for j in range(w // LANES):
        cs = slice(j * LANES, (j + 1) * LANES)
        y = yf_ref[0, :, cs] + yr_ref[0, :, cs]
        mean = _head_sum(y, bd) * (1.0 / HEAD_DIM)
        yc = y - mean
        var = _head_sum(yc * yc, bd) * (1.0 / HEAD_DIM)
        yn = yc * lax.rsqrt(var + D_GN_EPS) * lnw_ref[:, cs] + lnb_ref[:, cs]
        r = rkvk_ref[0, :, cs]
        k = rkvk_ref[0, :, w + j * LANES:w + (j + 1) * LANES]
        v = rkvk_ref[0, :, 2 * w + j * LANES:2 * w + (j + 1) * LANES]
        bonus = jnp.zeros_like(y)
        for z in range(2):
            a = a_ref[0, :, z * w + j * LANES:z * w + (j + 1) * LANES]
            k_dir = k * (1.0 + (a - 1.0) * ka_ref[:, cs])
            bonus = bonus + _head_sum(r * k_dir * rk_ref[:, cs], bd) * v
        o_ref[0, :, cs] = ((yn + bonus) * g_ref[0, :, cs]).astype(o_ref.dtype)


def _rwkv_out(y_f, y_r, rkvk, a, g, k_a, r_k, ln_w, ln_b, tm):
    b, t, w = y_f.shape
    bd = jnp.asarray(np.kron(np.eye(2), np.ones((HEAD_DIM, HEAD_DIM))), BF16)
    const = lambda bi, i: (0, 0)
    vec = lambda p: p.reshape(1, w).astype(F32)
    return pl.pallas_call(
        _rwkv_out_kernel,
        grid=(b, t // tm),
        in_specs=[
            pl.BlockSpec((1, tm, w), lambda bi, i: (bi, i, 0)),
            pl.BlockSpec((1, tm, w), lambda bi, i: (bi, i, 0)),
            pl.BlockSpec((1, tm, 4 * w), lambda bi, i: (bi, i, 0)),
            pl.BlockSpec((1, tm, 2 * w), lambda bi, i: (bi, i, 0)),
            pl.BlockSpec((1, tm, w), lambda bi, i: (bi, i, 0)),
            pl.BlockSpec((1, w), const),
            pl.BlockSpec((1, w), const),
            pl.BlockSpec((1, w), const),
            pl.BlockSpec((1, w), const),
            pl.BlockSpec((LANES, LANES), const),
        ],
        out_specs=pl.BlockSpec((1, tm, w), lambda bi, i: (bi, i, 0)),
        out_shape=jax.ShapeDtypeStruct((b, t, w), BF16),
        compiler_params=_cparams(("parallel", "arbitrary")),
        name="rwkv_out",
    )(y_f, y_r, rkvk, a, g, vec(k_a), vec(r_k), vec(ln_w), vec(ln_b), bd)


def _mod_vectors(mods_layer, b):
    d = D_MODEL
    lat = [mods_layer[:b, j * d:(j + 1) * d].reshape(b, 1, d) for j in range(6)]
    ctx = [jnp.broadcast_to(mods_layer[b:b + 1, j * d:(j + 1) * d].reshape(1, 1, d), (b, 1, d)) for j in range(6)]
    return lat, ctx


def _tile2(v):
    return jnp.concatenate([v, v]).astype(F32)


def _ab_layer(x_lat, x_ctx, mods_layer, layer, norm_mix, norm_ffn, wg, wu, wd, w_in, w_out,
              a_qn, a_kn, a_lam, a_subln, b_qn, b_kn, b_sink, rope_lat, rope_ctx, need_ctx):
    b, s_len, d = x_lat.shape
    (sh_l, sc_l, g_l, fsh_l, fsc_l, fg_l), (sh_c, sc_c, g_c, fsh_c, fsc_c, fg_c) = _mod_vectors(mods_layer, b)
    lambda_init = 0.8 - 0.6 * math.exp(-0.3 * layer)

    aw = A_HEADS * LANES
    aq, ak, av = w_in[:, :aw], w_in[:, aw:2 * aw], w_in[:, 2 * aw:3 * aw]
    bq = w_in[:, 3 * aw:4 * aw]
    bk = w_in[:, 4 * aw:4 * aw + LANES]
    bv = w_in[:, 4 * aw + LANES:4 * aw + 2 * LANES]
    dup = lambda w: jnp.concatenate([w[:, :64], w[:, :64], w[:, 64:], w[:, 64:]], axis=1)
    w_cat = jnp.concatenate([aq, ak, bq, dup(bk), dup(bv), av], axis=1).astype(BF16)
    scale = HEAD_DIM ** -0.5
    hg = jnp.stack([_tile2(a_qn) * (scale * LOG2E), _tile2(a_kn), _tile2(b_qn) * scale, _tile2(b_kn)]
                   + [jnp.zeros((LANES,), F32)] * 4)
    plan = ([("qk", 0, 0, j) for j in range(4)] + [("qk", 1, 0, 4 + j) for j in range(4)]
            + [("qk", 2, 0, 8 + j) for j in range(4)] + [("qk", 3, 0, 12 + j) for j in range(2)]
            + [("v", 0, 0, 14 + j) for j in range(2)] + [("vT", 0, 1, j) for j in range(4)])
    out_defs = [("rows", 16, BF16), ("vT", 4, BF16)]
    qkv_l, avT_l = _inproj(x_lat, sh_l, sc_l, norm_mix, w_cat, rope_lat[0], rope_lat[1], hg, plan, out_defs, tm=512)
    qkv_c, avT_c = _inproj(x_ctx, sh_c, sc_c, norm_mix, w_cat, rope_ctx[0], rope_ctx[1], hg, plan, out_defs,
                           tm=x_ctx.shape[1])

    lam_f = a_lam.astype(F32)
    lam = (jnp.exp(jnp.sum(lam_f[0] * lam_f[1])) - jnp.exp(jnp.sum(lam_f[2] * lam_f[3])) + lambda_init).reshape(1, 1)
    post = 1.0 - lambda_init
    ak_all = jnp.concatenate([qkv_c[:, :, 4 * LANES:8 * LANES], qkv_l[:, :, 4 * LANES:8 * LANES]], axis=1)
    avT_all = jnp.concatenate([avT_c, avT_l], axis=-1)
    a_lat = _diffattn(lam, qkv_l, 0, ak_all, 0, avT_all, a_subln, post)
    b_lat = _window_attn(b_sink, qkv_l, 8, 12, 14, qkv_c, with_win=True)
    x_lat = _out_ffn(x_lat, a_lat, b_lat, w_out, g_l, norm_ffn, fsh_l, fsc_l, fg_l, wg, wu, wd, tm=FFN_ROW_TILE)
    if need_ctx:
        a_ctx = _diffattn(lam, qkv_c, 0, qkv_c, 4, avT_c, a_subln, post)
        b_ctx = _window_attn(b_sink, qkv_c, 8, 12, 14, qkv_c, with_win=False)
        x_ctx = _out_ffn(x_ctx, a_ctx, b_ctx, w_out, g_c, norm_ffn, fsh_c, fsc_c, fg_c, wg, wu, wd, tm=256)
    return x_lat, x_ctx


def _pad_block(m):
    return jnp.concatenate([m, jnp.zeros(m.shape[:-1] + (LANES - m.shape[-1],), m.dtype)], axis=-1)


def _cd_layer(x_lat, x_ctx, mods_layer, norm_mix, norm_ffn, wg, wu, wd, w_in, w_out, c_qn, c_kn, c_rpb,
              d_mu, d_w0, d_w2, d_a0, d_a2, d_g2, d_k_k, d_k_a, d_r_k, d_ln_w, d_ln_b):
    b, s_len, d = x_lat.shape
    n_ctx = x_ctx.shape[1]
    (sh_l, sc_l, g_l, fsh_l, fsc_l, fg_l), (sh_c, sc_c, _, _, _, _) = _mod_vectors(mods_layer, b)
    w = D_WIDTH
    lora = lambda m: [_pad_block(m[..., 3 * w:3 * w + 64]), _pad_block(m[..., 3 * w + 64:3 * w + 128]),
                      _pad_block(m[..., 3 * w + 128:])]
    w_d = w_in[:, 3 * w:]
    w_cat = jnp.concatenate([w_in[:, :3 * w], w_d[:, :3 * w]] + lora(w_d), axis=1).astype(BF16)
    mu_p = jnp.concatenate([d_mu[:3 * w]] + lora(d_mu))
    scale = HEAD_DIM ** -0.5
    hg = jnp.stack([_tile2(c_qn) * scale, _tile2(c_kn)] + [jnp.zeros((LANES,), F32)] * 6)
    plan = ([("qk", 0, 0, j) for j in range(4)] + [("qk", 1, 0, 4 + j) for j in range(4)]
            + [("v", 0, 0, 8 + j) for j in range(4)] + [("raw", 0, 1, j) for j in range(15)])
    out_defs = [("rows", 12, BF16), ("rows", 15, F32)]
    no_rope = lambda n: (jnp.ones((n, LANES), F32), jnp.zeros((n, LANES), F32))
    qkv_l, dcols_l = _inproj(x_lat, sh_l, sc_l, norm_mix, w_cat, *no_rope(s_len), hg, plan, out_defs, tm=512)
    qkv_c, dcols_c = _inproj(x_ctx, sh_c, sc_c, norm_mix, w_cat, *no_rope(n_ctx), hg, plan, out_defs, tm=n_ctx)

    c_lat = _natten(qkv_l, qkv_c, _natten_bias(c_rpb))

    prep = functools.partial(_rwkv_prep, mu=mu_p, w0=d_w0, w2=d_w2, a0=d_a0, a2=d_a2, g2=d_g2, k_k=d_k_k, tm=256)
    rkvk_c, ld_c, a_c, _ = prep(dcols_c)
    rkvk_l, ld_l, a_l, g_l_gate = prep(dcols_l)
    s0 = jnp.zeros((b, 2, 4, LANES, LANES), F32)
    _, _, s_ctx = _rwkv_scan(rkvk_c, ld_c, a_c, d_k_a, s0, tile=min(SCAN_TILE, n_ctx))
    y_f, y_r, _ = _rwkv_scan(rkvk_l, ld_l, a_l, d_k_a, s_ctx, tile=SCAN_TILE)
    d_lat = _rwkv_out(y_f, y_r, rkvk_l, a_l, g_l_gate, d_k_a, d_r_k, d_ln_w, d_ln_b, tm=512)
    return _out_ffn(x_lat, c_lat, d_lat, w_out, g_l, norm_ffn, fsh_l, fsc_l, fg_l, wg, wu, wd, tm=FFN_ROW_TILE)


def kernel(x, c, ctx, c_ctx, ada_w, ada_b, norm_mix, norm_ffn, ffn_w_gate, ffn_w_up, ffn_w_down, ab_w_in, ab_w_out, a_q_norm, a_k_norm, a_lambda, a_subln, b_q_norm, b_k_norm, b_sink, cd_w_in, cd_w_out, c_q_norm, c_k_norm, c_rpb, d_mu, d_w0, d_w2, d_a0, d_a2, d_g2, d_k_k, d_k_a, d_r_k, d_ln_w, d_ln_b):
    b, s_len, d = x.shape
    n_ctx = ctx.shape[1]
    cond = jnp.concatenate([c, c_ctx[None, :], jnp.zeros((8 - b - 1, d), F32)], axis=0)
    mods = _adaln(cond, ada_w, ada_b)
    rope_lat = _rope_tables(s_len)
    rope_ctx = (jnp.ones((n_ctx, LANES), F32), jnp.zeros((n_ctx, LANES), F32))
    x_lat, x_ctx = x, ctx
    x_lat, x_ctx = _ab_layer(
        x_lat, x_ctx, mods[0], 0, norm_mix[0], norm_ffn[0], ffn_w_gate[0].astype(BF16), ffn_w_up[0].astype(BF16),
        ffn_w_down[0].astype(BF16), ab_w_in[0], ab_w_out[0].astype(BF16), a_q_norm[0], a_k_norm[0], a_lambda[0],
        a_subln[0], b_q_norm[0], b_k_norm[0], b_sink[0], rope_lat, rope_ctx, need_ctx=True)
    return _cd_layer(
        x_lat, x_ctx, mods[1], norm_mix[1], norm_ffn[1], ffn_w_gate[1].astype(BF16), ffn_w_up[1].astype(BF16),
        ffn_w_down[1].astype(BF16), cd_w_in[0], cd_w_out[0].astype(BF16), c_q_norm[0], c_k_norm[0], c_rpb[0],
        d_mu[0], d_w0[0], d_w2[0], d_a0[0], d_a2[0], d_g2[0], d_k_k[0], d_k_a[0], d_r_k[0], d_ln_w[0], d_ln_b[0])
```

```python
import functools
import math

import jax
import jax.numpy as jnp
import numpy as np
from jax import lax
from jax.experimental import pallas as pl
from jax.experimental.pallas import tpu as pltpu

F32 = jnp.float32
BF16 = jnp.bfloat16

D_MODEL = 1024
GRID_W = 64
HEAD_DIM = 64
ROPE_THETA = 10000.0
NORM_EPS = 1e-6
NEG_INF = -1e30
LANES = 128
A_HEADS = 4
WINDOW = 128
NA_ROWS = 8
NA_COLS = 16
D_HEADS = 8
D_WIDTH = 512
D_GN_EPS = 64e-5
LOG2E = math.log2(math.e)
VMEM_LIMIT = 56 * 1024 * 1024


def _cparams(sem):
    return pltpu.CompilerParams(dimension_semantics=sem, vmem_limit_bytes=VMEM_LIMIT)


def _dot(a, b):
    return jnp.dot(a, b, preferred_element_type=F32)


def _dot_nt(a, b):
    return lax.dot_general(a, b, (((1,), (1,)), ((), ())), preferred_element_type=F32)


def _split(x):
    hi = x.astype(BF16)
    lo = (x - hi.astype(F32)).astype(BF16)
    return hi, lo


def _dot3(a, b):
    ah, al = _split(a)
    bh, bl = _split(b)
    return _dot(ah, bh) + _dot(al, bh) + _dot(ah, bl)


def _dot3_nt(a, b):
    ah, al = _split(a)
    bh, bl = _split(b)
    return _dot_nt(ah, bh) + _dot_nt(al, bh) + _dot_nt(ah, bl)


def _dot2_exact_rhs(a, b_bf16):
    ah, al = _split(a)
    return _dot(ah, b_bf16) + _dot(al, b_bf16)


def _norm_mod(x, gain, shift, scale):
    ms = jnp.mean(x * x, axis=-1, keepdims=True)
    y = x * lax.rsqrt(ms + NORM_EPS) * gain
    return y * (1.0 + scale) + shift


def _adaln_kernel(c_ref, w_ref, b_ref, o_ref):
    c = c_ref[...]
    o_ref[0] = _dot3(jax.nn.silu(c), w_ref[0]) + b_ref[0]


def _adaln(cond, ada_w, ada_b):
    depth, d, n = ada_w.shape
    tn = 1536
    return pl.pallas_call(
        _adaln_kernel,
        grid=(depth, n // tn),
        in_specs=[
            pl.BlockSpec((8, d), lambda l, j: (0, 0)),
            pl.BlockSpec((1, d, tn), lambda l, j: (l, 0, j)),
            pl.BlockSpec((1, 1, tn), lambda l, j: (l, 0, j)),
        ],
        out_specs=pl.BlockSpec((1, 8, tn), lambda l, j: (l, 0, j)),
        out_shape=jax.ShapeDtypeStruct((depth, 8, n), F32),
        compiler_params=_cparams(("arbitrary", "arbitrary")),
        name="adaln",
    )(cond, ada_w, ada_b.reshape(depth, 1, n))


def _head_norm_rope(y, bd, hg, cos, sin):
    ss = _dot2_exact_rhs(y * y, bd)
    y = y * lax.rsqrt(ss * (1.0 / HEAD_DIM) + NORM_EPS) * hg
    lane = lax.broadcasted_iota(jnp.int32, y.shape, 1)
    first_half = (lane % 32) < 16
    partner = jnp.where(first_half, pltpu.roll(y, LANES - 16, axis=1), pltpu.roll(y, 16, axis=1))
    return y * cos + partner * sin


def _inproj_kernel(x_ref, sh_ref, sc_ref, gain_ref, w_ref, cos_ref, sin_ref, hg_ref, bd_ref, *out_refs, plan):
    h = _norm_mod(x_ref[0], gain_ref[...], sh_ref[0], sc_ref[0]).astype(BF16)
    bd = bd_ref[...]
    cos = cos_ref[...]
    sin = sin_ref[...]
    n_blocks = len(plan)
    project = lambda j: _dot(h, w_ref[:, j * LANES:(j + min(2, n_blocks - j)) * LANES])
    y_next = project(0)
    for j0 in range(0, n_blocks, 2):
        width = min(2, n_blocks - j0) * LANES
        y2 = y_next
        if j0 + 2 < n_blocks:
            y_next = project(j0 + 2)
        for jj in range(width // LANES):
            kind, hg_idx, out_idx, out_blk = plan[j0 + jj]
            y = y2[:, jj * LANES:(jj + 1) * LANES]
            o_ref = out_refs[out_idx]
            if kind == "qk":
                y = _head_norm_rope(y, bd, hg_ref[hg_idx:hg_idx + 1, :], cos, sin)
                o_ref[0, :, out_blk * LANES:(out_blk + 1) * LANES] = y.astype(o_ref.dtype)
            elif kind == "vT":
                o_ref[0, out_blk] = y.T.astype(o_ref.dtype)
            else:
                o_ref[0, :, out_blk * LANES:(out_blk + 1) * LANES] = y.astype(o_ref.dtype)


def _inproj(x, shift, scale, gain, w, cos, sin, hg, plan, out_defs, tm):
    b, t, d = x.shape
    n = w.shape[1]
    bd = jnp.asarray(np.kron(np.eye(2), np.ones((HEAD_DIM, HEAD_DIM))), BF16)
    out_shapes, out_specs = [], []
    for kind, nblk, dt in out_defs:
        if kind == "rows":
            out_shapes.append(jax.ShapeDtypeStruct((b, t, nblk * LANES), dt))
            out_specs.append(pl.BlockSpec((1, tm, nblk * LANES), lambda bi, i: (bi, i, 0)))
        else:
            out_shapes.append(jax.ShapeDtypeStruct((b, nblk, LANES, t), dt))
            out_specs.append(pl.BlockSpec((1, nblk, LANES, tm), lambda bi, i: (bi, 0, 0, i)))
    return pl.pallas_call(
        functools.partial(_inproj_kernel, plan=tuple(plan)),
        grid=(b, t // tm),
        in_specs=[
            pl.BlockSpec((1, tm, d), lambda bi, i: (bi, i, 0)),
            pl.BlockSpec((1, 1, d), lambda bi, i: (bi, 0, 0)),
            pl.BlockSpec((1, 1, d), lambda bi, i: (bi, 0, 0)),
            pl.BlockSpec((1, d), lambda bi, i: (0, 0)),
            pl.BlockSpec((d, n), lambda bi, i: (0, 0)),
            pl.BlockSpec((tm, LANES), lambda bi, i: (i, 0)),
            pl.BlockSpec((tm, LANES), lambda bi, i: (i, 0)),
            pl.BlockSpec(hg.shape, lambda bi, i: (0, 0)),
            pl.BlockSpec((LANES, LANES), lambda bi, i: (0, 0)),
        ],
        out_specs=out_specs,
        out_shape=out_shapes,
        compiler_params=_cparams(("parallel", "arbitrary")),
        name="inproj",
    )(x, shift, scale, gain.reshape(1, d), w, cos, sin, hg, bd)


def _rope_tables(n_tokens):
    axis_dim = HEAD_DIM // 2
    freqs = ROPE_THETA ** (-jnp.arange(0, axis_dim, 2, dtype=F32) / axis_dim)
    t = jnp.arange(n_tokens, dtype=jnp.int32)
    lane = np.arange(LANES)
    d = lane % HEAD_DIM
    use_col = (d // 32) == 1
    f_idx = (d % 32) % 16
    sign = np.where((d % 32) < 16, -1.0, 1.0).astype(np.float32)
    pos = jnp.where(use_col[None, :], (t % GRID_W)[:, None], (t // GRID_W)[:, None]).astype(F32)
    ang = pos * freqs[f_idx][None, :]
    return jnp.cos(ang), jnp.sin(ang) * sign[None, :]


def _diffattn_kernel(lam_ref, q_ref, k_ref, vT_ref, subln_ref, o_ref, *s_refs, tq, tk, n_tiles, post_scale):
    n_q = q_ref.shape[1] // tq
    lane = lax.broadcasted_iota(jnp.int32, (tq, LANES), 1)
    sub = DIFF_SUB_BLOCK
    n_sub = tk // sub
    n_buf = len(s_refs)
    ahead = n_buf - 1

    def query_weights(qt):
        q = q_ref[0, qt * tq:(qt + 1) * tq, :]
        zero = jnp.zeros_like(q)
        qq = jnp.concatenate([jnp.where(lane < HEAD_DIM, q, zero), jnp.where(lane >= HEAD_DIM, q, zero)], axis=0)
        return qq.astype(F32).T.astype(BF16)

    qq_t = [query_weights(qt) for qt in range(n_q)]

    def scores(qt, i, s_ref):
        s = _dot(k_ref[0, i * tk:(i + 1) * tk, :], qq_t[qt])
        s_ref[...] = s
        return jnp.max(s, axis=0, keepdims=True)

    def finish(qt, l, acc):
        o = acc / l
        oT = o[:, :tq] - lam_ref[0, 0] * o[:, tq:]
        ms = jnp.mean(oT * oT, axis=0, keepdims=True)
        oT = oT * lax.rsqrt(ms + NORM_EPS) * subln_ref[...] * post_scale
        o_ref[0, qt * tq:(qt + 1) * tq, :] = oT.T.astype(o_ref.dtype)

    items = [(qt, i) for qt in range(n_q) for i in range(n_tiles)]
    cmaxes = {}
    for n in range(min(ahead, len(items))):
        cmaxes[n] = scores(*items[n], s_refs[n % n_buf])
    state = None
    for n, (qt, i) in enumerate(items):
        if i == 0:
            state = (jnp.full((1, 2 * tq), -jnp.inf, F32), jnp.zeros((1, 2 * tq), F32),
                     jnp.zeros((LANES, 2 * tq), F32))
        m, l, acc = state
        m_new = jnp.maximum(m, cmaxes.pop(n))
        alpha = jnp.exp2(m - m_new)
        s_ref = s_refs[n % n_buf]
        if n + ahead < len(items):
            cmaxes[n + ahead] = scores(*items[n + ahead], s_refs[(n + ahead) % n_buf])
        pv, psum = None, None
        for j in range(n_sub):
            p = jnp.exp2(s_ref[j * sub:(j + 1) * sub, :] - m_new)
            ps = jnp.sum(p, axis=0, keepdims=True)
            d = _dot(vT_ref[0, 0, :, i * tk + j * sub:i * tk + (j + 1) * sub], p.astype(BF16))
            psum = ps if psum is None else psum + ps
            pv = d if pv is None else pv + d
        state = (m_new, alpha * l + psum, alpha * acc + pv)
        if i == n_tiles - 1:
            finish(qt, state[1], state[2])


DIFF_KEY_TILE = 1280
DIFF_SUB_BLOCK = 256
DIFF_LOOKAHEAD = 2
DIFF_Q_TILES_PER_STEP = 2


def _diffattn(lam, q_arr, q_blk0, k_arr, k_blk0, vT, subln, post_scale, tq=128):
    b, t_q, _ = q_arr.shape
    t_k = k_arr.shape[1]
    tk = DIFF_KEY_TILE if (t_k % DIFF_KEY_TILE == 0 and t_k > DIFF_KEY_TILE) else 256
    rows = tq * DIFF_Q_TILES_PER_STEP
    return pl.pallas_call(
        functools.partial(_diffattn_kernel, tq=tq, tk=tk, n_tiles=t_k // tk, post_scale=post_scale),
        grid=(b, A_HEADS, t_q // rows),
        in_specs=[
            pl.BlockSpec(memory_space=pltpu.SMEM),
            pl.BlockSpec((1, rows, LANES), lambda bi, h, i: (bi, i, q_blk0 + h)),
            pl.BlockSpec((1, t_k, LANES), lambda bi, h, i: (bi, 0, k_blk0 + h)),
            pl.BlockSpec((1, 1, LANES, t_k), lambda bi, h, i: (bi, h, 0, 0)),
            pl.BlockSpec((LANES, 1), lambda bi, h, i: (0, 0)),
        ],
        out_specs=pl.BlockSpec((1, rows, LANES), lambda bi, h, i: (bi, i, h)),
        out_shape=jax.ShapeDtypeStruct((b, t_q, A_HEADS * LANES), BF16),
        scratch_shapes=[pltpu.VMEM((tk, 2 * tq), F32)] * (DIFF_LOOKAHEAD + 1),
        compiler_params=_cparams(("parallel", "arbitrary", "arbitrary")),
        name="diffattn",
    )(lam, q_arr, k_arr, vT, subln.reshape(LANES, 1))


def _window_kernel(sink_ref, q_ref, kc_ref, vc_ref, *rest, with_win, n_blocks):
    if with_win:
        kp_ref, k0_ref, kn_ref, vp_ref, v0_ref, vn_ref, o_ref = rest
    else:
        (o_ref,) = rest
    i = pl.program_id(1)
    tq = q_ref.shape[1]
    n_ctx = kc_ref.shape[1]
    lane = lax.broadcasted_iota(jnp.int32, (1, LANES), 1)
    lo = lane < HEAD_DIM
    if with_win:
        r = lax.broadcasted_iota(jnp.int32, (tq, tq), 0)
        c = lax.broadcasted_iota(jnp.int32, (tq, tq), 1)
        ok_prev = jnp.logical_and(c >= r, i > 0)
        ok_next = jnp.logical_and(c <= r, i < n_blocks - 1)
        valid = jnp.concatenate([ok_prev, jnp.ones((tq, tq), jnp.bool_), ok_next,
                                 jnp.ones((tq, n_ctx), jnp.bool_)], axis=1)
    scores, vcats = [], []
    for g in range(2):
        gs = slice(g * LANES, (g + 1) * LANES)
        if with_win:
            kcat = jnp.concatenate([kp_ref[0, :, gs], k0_ref[0, :, gs], kn_ref[0, :, gs], kc_ref[0, :, gs]], axis=0)
            vcat = jnp.concatenate([vp_ref[0, :, gs], v0_ref[0, :, gs], vn_ref[0, :, gs], vc_ref[0, :, gs]], axis=0)
        else:
            kcat = kc_ref[0, :, gs]
            vcat = vc_ref[0, :, gs]
        q_rows = []
        for cb in range(2):
            qb = q_ref[0, :, (g * 2 + cb) * LANES:(g * 2 + cb + 1) * LANES]
            zq = jnp.zeros_like(qb)
            q_rows += [jnp.where(lo, qb, zq), jnp.where(lo, zq, qb)]
        scores.append(_dot_nt(jnp.concatenate(q_rows, axis=0), kcat))
        vcats.append(vcat)
    for g in range(2):
        s = scores[g]
        if with_win:
            s = jnp.where(jnp.concatenate([valid] * 4, axis=0), s, NEG_INF)
        snk = jnp.concatenate([jnp.broadcast_to(sink_ref[g * 4 + h:g * 4 + h + 1, 0:1], (tq, 1)) for h in range(4)],
                              axis=0)
        m = jnp.maximum(jnp.max(s, axis=-1, keepdims=True), snk)
        e = jnp.exp(s - m)
        denom = jnp.sum(e, axis=-1, keepdims=True) + jnp.exp(snk - m)
        o = _dot(e.astype(BF16), vcats[g]) / denom
        for cb in range(2):
            out = jnp.where(lo, o[(2 * cb) * tq:(2 * cb + 1) * tq], o[(2 * cb + 1) * tq:(2 * cb + 2) * tq])
            o_ref[0, :, (g * 2 + cb) * LANES:(g * 2 + cb + 1) * LANES] = out.astype(o_ref.dtype)


def _window_attn(sink, q_arr, q_blk0, k_blk0, v_blk0, ctx_arr, with_win):
    b, t_q, _ = q_arr.shape
    n_ctx = ctx_arr.shape[1]
    tq = WINDOW
    nb = t_q // tq
    sink_tab = jnp.broadcast_to(sink.astype(F32).reshape(8, 1), (8, LANES))
    in_specs = [
        pl.BlockSpec((8, LANES), lambda bi, i: (0, 0)),
        pl.BlockSpec((1, tq, 4 * LANES), lambda bi, i: (bi, i, q_blk0 // 4)),
        pl.BlockSpec((1, n_ctx, 2 * LANES), lambda bi, i: (bi, 0, k_blk0 // 2)),
        pl.BlockSpec((1, n_ctx, 2 * LANES), lambda bi, i: (bi, 0, v_blk0 // 2)),
    ]
    args = [sink_tab, q_arr, ctx_arr, ctx_arr]
    if with_win:
        for blk0 in (k_blk0, v_blk0):
            in_specs += [
                pl.BlockSpec((1, tq, 2 * LANES), lambda bi, i, c=blk0 // 2: (bi, jnp.maximum(i - 1, 0), c)),
                pl.BlockSpec((1, tq, 2 * LANES), lambda bi, i, c=blk0 // 2: (bi, i, c)),
                pl.BlockSpec((1, tq, 2 * LANES), lambda bi, i, c=blk0 // 2: (bi, jnp.minimum(i + 1, nb - 1), c)),
            ]
            args += [q_arr, q_arr, q_arr]
    return pl.pallas_call(
        functools.partial(_window_kernel, with_win=with_win, n_blocks=nb),
        grid=(b, nb),
        in_specs=in_specs,
        out_specs=pl.BlockSpec((1, tq, 4 * LANES), lambda bi, i: (bi, i, 0)),
        out_shape=jax.ShapeDtypeStruct((b, t_q, 4 * LANES), BF16),
        compiler_params=_cparams(("parallel", "arbitrary")),
        name="window_attn",
    )(*args)


FFN_ROW_TILE = 512


def _out_ffn_kernel(x_ref, a_ref, b_ref, woa_ref, wob_ref, gate_ref, ng_ref, fsh_ref, fsc_ref, fg_ref,
                    wg_ref, wu_ref, wd_ref, o_ref):
    o = _dot(a_ref[0], woa_ref[...]) + _dot(b_ref[0], wob_ref[...])
    x1 = x_ref[0] + gate_ref[0] * o
    h = _norm_mod(x1, ng_ref[...], fsh_ref[0], fsc_ref[0]).astype(BF16)
    act = (jax.nn.silu(_dot(h, wg_ref[...])) * _dot(h, wu_ref[...])).astype(BF16)
    o_ref[0] = x1 + fg_ref[0] * _dot(act, wd_ref[...])


def _out_ffn(x, mix_a, mix_b, wo, gate, ng, fsh, fsc, fg, wg, wu, wd, tm):
    b, t, d = x.shape
    na, nb_ = mix_a.shape[-1], mix_b.shape[-1]
    dff = wg.shape[1]
    row = lambda bi, i: (bi, i, 0)
    vec = lambda bi, i: (bi, 0, 0)
    const = lambda bi, i: (0, 0)
    resident = functools.partial(pl.BlockSpec, index_map=const, pipeline_mode=pl.Buffered(1))
    return pl.pallas_call(
        _out_ffn_kernel,
        grid=(b, t // tm),
        in_specs=[
            pl.BlockSpec((1, tm, d), row),
            pl.BlockSpec((1, tm, na), row),
            pl.BlockSpec((1, tm, nb_), row),
            resident((na, d)),
            resident((nb_, d)),
            pl.BlockSpec((1, 1, d), vec),
            resident((1, d)),
            pl.BlockSpec((1, 1, d), vec),
            pl.BlockSpec((1, 1, d), vec),
            pl.BlockSpec((1, 1, d), vec),
            resident((d, dff)),
            resident((d, dff)),
            resident((dff, d)),
        ],
        out_specs=pl.BlockSpec((1, tm, d), row),
        out_shape=jax.ShapeDtypeStruct((b, t, d), F32),
        compiler_params=_cparams(("parallel", "arbitrary")),
        name="out_ffn",
    )(x, mix_a, mix_b, wo[:na], wo[na:], gate, ng.reshape(1, d), fsh, fsc, fg, wg, wu, wd)


NA_ROWS_PER_STEP = 8


def _natten_kernel(q_ref, k_ref, v_ref, kc_ref, vc_ref, bias_ref, o_ref, *, n_rows):
    step = pl.program_id(2)
    n_win = NA_ROWS * GRID_W
    lane = lax.broadcasted_iota(jnp.int32, (1, LANES), 1)
    lo = lane < HEAD_DIM
    keeps = (lo, jnp.logical_not(lo))
    q_all = q_ref[0]
    zq = jnp.zeros_like(q_all)
    q_half = [jnp.where(keep, q_all, zq) for keep in keeps]
    s_ctx = [_dot_nt(qh, kc_ref[0]) for qh in q_half]
    row_info, s_win = [], []
    for rr in range(NA_ROWS_PER_STEP):
        i = step * NA_ROWS_PER_STEP + rr
        r0 = jnp.clip(i - NA_ROWS // 2, 0, n_rows - NA_ROWS)
        koff = pl.multiple_of(r0 * GRID_W, GRID_W)
        row_info.append((i - r0, koff))
        k_win = k_ref[0, pl.ds(koff, n_win), :]
        for half in range(2):
            s_win.append(_dot_nt(q_half[half][rr * GRID_W:(rr + 1) * GRID_W], k_win))
    e_win, e_ctx, denom = [], [[], []], [[], []]
    for rr in range(NA_ROWS_PER_STEP):
        qs = slice(rr * GRID_W, (rr + 1) * GRID_W)
        for half in range(2):
            sw = s_win[2 * rr + half] + bias_ref[row_info[rr][0], half]
            sc = s_ctx[half][qs]
            m = jnp.maximum(jnp.max(sw, axis=-1, keepdims=True), jnp.max(sc, axis=-1, keepdims=True))
            ew, ec = jnp.exp(sw - m), jnp.exp(sc - m)
            e_win.append(ew.astype(BF16))
            e_ctx[half].append(ec.astype(BF16))
            denom[half].append(jnp.sum(ew, axis=-1, keepdims=True) + jnp.sum(ec, axis=-1, keepdims=True))
    vc = vc_ref[0]
    zc = jnp.zeros_like(vc)
    o_ctx = [_dot(jnp.concatenate(e_ctx[half], axis=0), jnp.where(keeps[half], vc, zc)) for half in range(2)]
    for rr in range(NA_ROWS_PER_STEP):
        qs = slice(rr * GRID_W, (rr + 1) * GRID_W)
        v_win = v_ref[0, pl.ds(row_info[rr][1], n_win), :]
        zv = jnp.zeros_like(v_win)
        out = None
        for half in range(2):
            o = (_dot(e_win[2 * rr + half], jnp.where(keeps[half], v_win, zv)) + o_ctx[half][qs]) / denom[half][rr]
            out = o if out is None else out + o
        o_ref[0, qs, :] = out.astype(o_ref.dtype)


def _natten_bias(rpb):
    n_heads = rpb.shape[0]
    qj = np.arange(GRID_W)[:, None]
    kj = np.arange(GRID_W)[None, :]
    cstart = np.clip(qj - NA_COLS // 2, 0, GRID_W - NA_COLS)
    in_win = (kj >= cstart) & (kj < cstart + NA_COLS)
    pad = GRID_W - NA_COLS
    rpb_p = jnp.pad(rpb.astype(F32), ((0, 0), (0, 0), (pad, pad)))
    toep = jnp.stack([rpb_p[:, :, pad + NA_COLS - 1 - q:pad + NA_COLS - 1 - q + GRID_W] for q in range(GRID_W)],
                     axis=2)
    toep = jnp.where(in_win[None, None], toep, NEG_INF)
    per_class = [jnp.moveaxis(toep[:, NA_ROWS - 1 - d:2 * NA_ROWS - 1 - d], 1, 2) for d in range(NA_ROWS)]
    return jnp.stack(per_class).reshape(NA_ROWS, n_heads, GRID_W, NA_ROWS * GRID_W)


def _natten(qkv, qkv_ctx, bias):
    b, s_len, _ = qkv.shape
    n_ctx = qkv_ctx.shape[1]
    n_rows = s_len // GRID_W
    tq = NA_ROWS_PER_STEP * GRID_W
    return pl.pallas_call(
        functools.partial(_natten_kernel, n_rows=n_rows),
        grid=(b, 4, n_rows // NA_ROWS_PER_STEP),
        in_specs=[
            pl.BlockSpec((1, tq, LANES), lambda bi, hp, i: (bi, i, hp)),
            pl.BlockSpec((1, s_len, LANES), lambda bi, hp, i: (bi, 0, 4 + hp)),
            pl.BlockSpec((1, s_len, LANES), lambda bi, hp, i: (bi, 0, 8 + hp)),
            pl.BlockSpec((1, n_ctx, LANES), lambda bi, hp, i: (bi, 0, 4 + hp)),
            pl.BlockSpec((1, n_ctx, LANES), lambda bi, hp, i: (bi, 0, 8 + hp)),
            pl.BlockSpec((NA_ROWS, 2, GRID_W, NA_ROWS * GRID_W), lambda bi, hp, i: (0, hp, 0, 0)),
        ],
        out_specs=pl.BlockSpec((1, tq, LANES), lambda bi, hp, i: (bi, i, hp)),
        out_shape=jax.ShapeDtypeStruct((b, s_len, 4 * LANES), BF16),
        compiler_params=_cparams(("parallel", "arbitrary", "arbitrary")),
        name="natten",
    )(qkv, qkv, qkv, qkv_ctx, qkv_ctx, bias)


def _head_sum(x, bd):
    return _dot2_exact_rhs(x, bd)


def _rwkv_prep_kernel(x_ref, xp_ref, xn_ref, mu_ref, w0_ref, w2_ref, a0_ref, a2_ref, g2_ref, kk_ref, bd_ref,
                      rkvk_ref, ld_ref, a_ref, g_ref):
    i = pl.program_id(1)
    n_t = pl.num_programs(1)
    x = x_ref[0]
    tm = x.shape[0]
    prev_row = jnp.where(i > 0, xp_ref[0, 7:8, :], 0.0)
    next_row = jnp.where(i < n_t - 1, xn_ref[0, 0:1, :], 0.0)
    rows = lax.broadcasted_iota(jnp.int32, (tm, 1), 0)
    x_prev = jnp.where(rows == 0, prev_row, pltpu.roll(x, 1, axis=0))
    x_next = jnp.where(rows == tm - 1, next_row, pltpu.roll(x, tm - 1, axis=0))
    xs = x + mu_ref[...] * (0.5 * (x_prev + x_next) - x)
    w = D_WIDTH
    k = xs[:, w:2 * w]
    rkvk_ref[0, :, 0:3 * w] = xs[:, 0:3 * w]
    kk = k * kk_ref[...]
    nrm = jnp.sqrt(jnp.concatenate([_head_sum(kk[:, j * LANES:(j + 1) * LANES] ** 2, bd_ref[...])
                                    for j in range(w // LANES)], axis=1))
    rkvk_ref[0, :, 3 * w:4 * w] = kk / jnp.maximum(nrm, 1e-12)
    wd = xs[:, 3 * w:3 * w + LANES]
    ad = xs[:, 3 * w + LANES:3 * w + 2 * LANES]
    gd = xs[:, 3 * w + 2 * LANES:3 * w + 3 * LANES]
    wraw = -jax.nn.softplus(-(w0_ref[...] + _dot3(jnp.tanh(wd), w2_ref[...]))) - 0.5
    ld_ref[0] = -jnp.exp(wraw)
    a_ref[0] = jax.nn.sigmoid(a0_ref[...] + _dot3(ad, a2_ref[...]))
    g_ref[0] = _dot3(jax.nn.sigmoid(gd), g2_ref[...])


def _rwkv_prep(dcols, mu, w0, w2, a0, a2, g2, k_k, tm):
    b, t, n = dcols.shape
    w = D_WIDTH
    bd = jnp.asarray(np.kron(np.eye(2), np.ones((HEAD_DIM, HEAD_DIM))), BF16)
    pad_rows = lambda m, r0: jnp.zeros((LANES, m.shape[1]), F32).at[r0:r0 + m.shape[0]].set(m)
    w2p = jnp.concatenate([pad_rows(w2[0], 0), pad_rows(w2[1], 32)], axis=1)
    a2p = jnp.concatenate([pad_rows(a2[0], 0), pad_rows(a2[1], 32)], axis=1)
    g2p = pad_rows(g2, 0)
    const = lambda bi, i: (0, 0)
    t8 = tm // 8
    return pl.pallas_call(
        _rwkv_prep_kernel,
        grid=(b, t // tm),
        in_specs=[
            pl.BlockSpec((1, tm, n), lambda bi, i: (bi, i, 0)),
            pl.BlockSpec((1, 8, n), lambda bi, i: (bi, jnp.maximum(i * t8 - 1, 0), 0)),
            pl.BlockSpec((1, 8, n), lambda bi, i: (bi, jnp.minimum((i + 1) * t8, t // 8 - 1), 0)),
            pl.BlockSpec((1, n), const),
            pl.BlockSpec((1, 2 * w), const),
            pl.BlockSpec((LANES, 2 * w), const),
            pl.BlockSpec((1, 2 * w), const),
            pl.BlockSpec((LANES, 2 * w), const),
            pl.BlockSpec((LANES, w), const),
            pl.BlockSpec((1, w), const),
            pl.BlockSpec((LANES, LANES), const),
        ],
        out_specs=[
            pl.BlockSpec((1, tm, 4 * w), lambda bi, i: (bi, i, 0)),
            pl.BlockSpec((1, tm, 2 * w), lambda bi, i: (bi, i, 0)),
            pl.BlockSpec((1, tm, 2 * w), lambda bi, i: (bi, i, 0)),
            pl.BlockSpec((1, tm, w), lambda bi, i: (bi, i, 0)),
        ],
        out_shape=[
            jax.ShapeDtypeStruct((b, t, 4 * w), F32),
            jax.ShapeDtypeStruct((b, t, 2 * w), F32),
            jax.ShapeDtypeStruct((b, t, 2 * w), F32),
            jax.ShapeDtypeStruct((b, t, w), F32),
        ],
        compiler_params=_cparams(("parallel", "arbitrary")),
        name="rwkv_prep",
    )(dcols, dcols, dcols, mu.reshape(1, n), w0.reshape(1, 2 * w), w2p, a0.reshape(1, 2 * w), a2p, g2p,
      k_k.reshape(1, w), bd)


SCAN_CHUNK = 64
SCAN_TILE = 512


def _split3(x):
    hi = x.astype(BF16)
    r1 = x - hi.astype(F32)
    mid = r1.astype(BF16)
    lo = (r1 - mid.astype(F32)).astype(BF16)
    return hi, mid, lo


def _mm(a, b, precise):
    return _dot3(a, b) if precise else _dot(a.astype(BF16), b.astype(BF16))


def _mm_nt(a, b, precise):
    return _dot3_nt(a, b) if precise else _dot_nt(a.astype(BF16), b.astype(BF16))


def _blockwise_mm(x, y, bd_mask, precise):
    n_blk = x.shape[1] // SCAN_CHUNK
    expand = lambda m: jnp.where(bd_mask, jnp.concatenate([m] * n_blk, axis=0), jnp.zeros((), m.dtype))
    if not precise:
        return _dot(x.astype(BF16), expand(y.astype(BF16)))
    xh, xl = _split(x)
    yh, yl = _split(y)
    ybd = expand(yh)
    return _dot(xh, ybd) + _dot(xl, ybd) + _dot(xh, expand(yl))


SCAN_PRECISE_INVERSE = False
SCAN_PRECISE_STATE = False
SCAN_PRECISE_OTHER = False


def _rwkv_scan_kernel(*refs, n_chunks):
    fwd_refs, rev_refs = refs[0:6], refs[6:12]
    ka_ref, s0_ref = refs[12:14]
    y_refs = refs[14:16]
    sf_ref, state_ref, t64_ref, t128_ref = refs[16:20]
    it = pl.program_id(2)
    last = pl.num_programs(2) - 1
    c = SCAN_CHUNK
    n_units = n_chunks // 2

    @pl.when(it == 0)
    def _():
        state_ref[...] = s0_ref[0, :, 0]
        t64_ref[...] = jnp.zeros_like(t64_ref)
        t128_ref[...] = jnp.zeros_like(t128_ref)

    ri = lax.broadcasted_iota(jnp.int32, (c, c), 0)
    ci = lax.broadcasted_iota(jnp.int32, (c, c), 1)
    eye = (ri == ci).astype(F32)
    tile_l = lambda m, n: jnp.concatenate([m] * n, axis=1)
    eye4 = tile_l(eye, 4)
    blk = lambda n: ((ri // n) == (ci // n)).astype(F32)
    m8, m16, m32 = tile_l(blk(8), 4), tile_l(blk(16), 4), tile_l(blk(32), 4)
    r4 = lax.broadcasted_iota(jnp.int32, (4 * c, 4 * c), 0)
    c4 = lax.broadcasted_iota(jnp.int32, (4 * c, 4 * c), 1)
    bd4 = (r4 // c) == (c4 // c)
    lane = lax.broadcasted_iota(jnp.int32, (1, LANES), 1)
    hm_lo = (lane < HEAD_DIM).astype(F32)
    hm_hi = 1.0 - hm_lo
    r2 = lax.broadcasted_iota(jnp.int32, (LANES, LANES), 0)
    c2 = lax.broadcasted_iota(jnp.int32, (LANES, LANES), 1)
    bd128 = ((r2 // HEAD_DIM) == (c2 // HEAD_DIM)).astype(F32)
    eye128 = (r2 == c2).astype(F32)
    ka = ka_ref[...]
    vstack = lambda *xs: jnp.concatenate(xs, axis=0)
    hstack = lambda *xs: jnp.concatenate(xs, axis=1)
    split_heads = lambda m: vstack(m * hm_lo, m * hm_hi)
    po = SCAN_PRECISE_OTHER

    rows = [slice(j * c, (j + 1) * c) for j in range(n_chunks)]
    masks = []
    for z in range(2):
        strict = ((ci > ri) if z == 1 else (ci < ri)).astype(F32)
        incl = strict + eye
        masks.append(dict(strict2=tile_l(strict, 2), incl2=tile_l(incl, 2), tri_b=incl.astype(BF16)))

    states = [state_ref[0], state_ref[1]]
    pending = {}

    def chain_step(step):
        for z in range(2):
            j = step if z == 0 else n_chunks - 1 - step
            idx = z * n_chunks + j
            s_in = states[z]
            s_new = _mm(s_in, t128_ref[idx, 0], SCAN_PRECISE_STATE) + t128_ref[idx, 1]
            states[z] = jnp.where(it > 0, s_new, s_in)
            wr = _mm_nt(vstack(t64_ref[idx, 0], t64_ref[idx, 1]), s_in, SCAN_PRECISE_STATE)
            pending[(step, z)] = (idx, j, wr)

    def chain_outputs(step):
        for z in range(2):
            idx, j, wr = pending.pop((step, z))
            u_full = wr[:c] + t64_ref[idx, 2]
            y_refs[z][0, rows[j], :] = wr[c:] + t64_ref[idx, 3] + _mm(t64_ref[idx, 4], split_heads(u_full), po)

    chain_iter = iter(range(n_chunks + 1))

    def chain_slot():
        k = next(chain_iter, None)
        if k is None:
            return
        if k > 0:
            chain_outputs(k - 1)
        if k < n_chunks:
            chain_step(k)

    cums = []
    for z, in_refs in enumerate((fwd_refs, rev_refs)):
        h3 = _split3(hstack(*[in_refs[4][0, rw, :] for rw in rows]))
        tri_b = masks[z]["tri_b"]
        cums.append(_dot(tri_b, h3[0]) + _dot(tri_b, h3[1]) + _dot(tri_b, h3[2]))
    chain_slot()

    chunks = []
    for z, in_refs in enumerate((fwd_refs, rev_refs)):
        r_ref, k_ref, v_ref, kk_ref, ld_ref, a_ref = in_refs
        for j, rw in enumerate(rows):
            r, k, v, kk = r_ref[0, rw, :], k_ref[0, rw, :], v_ref[0, rw, :], kk_ref[0, rw, :]
            ld, a = ld_ref[0, rw, :], a_ref[0, rw, :]
            cum = cums[z][:, j * LANES:(j + 1) * LANES]
            pinv = jnp.exp(-cum)
            p_end = jnp.exp(cum[0:1, :] if z == 1 else cum[c - 1:c, :])
            r_t = r * jnp.exp(cum)
            a_t = -kk * jnp.exp(cum - ld)
            b_t = kk * a * pinv
            k_t = k * (1.0 + (a - 1.0) * ka) * pinv
            chunks.append(dict(z=z, j=j, r_t=r_t, a_t=a_t, b_t=b_t, k_t=k_t, v=v, p_end=p_end))
    for ch in chunks:
        g = _mm_nt(vstack(ch["a_t"], ch["r_t"]), vstack(split_heads(ch["b_t"]), split_heads(ch["k_t"])), po)
        mk = masks[ch["z"]]
        ch.update(l_ab=g[:c, :2 * c] * mk["strict2"], l_ak=g[:c, 2 * c:] * mk["strict2"],
                  m_rb=g[c:, :2 * c] * mk["incl2"], m_rk=g[c:, 2 * c:] * mk["incl2"])
    chain_slot()
    for ch in chunks:
        lv_y0 = _mm(vstack(ch["l_ak"], ch["m_rk"]), split_heads(ch["v"]), po)
        ch.update(lv=lv_y0[:c], y0=lv_y0[c:])
    chain_slot()

    mm4 = lambda x, y: _blockwise_mm(x, y, bd4, SCAN_PRECISE_INVERSE)
    l4s = [hstack(chunks[2 * u]["l_ab"], chunks[2 * u + 1]["l_ab"]) for u in range(2 * n_units)]
    p1s = [l4 * m8 for l4 in l4s]
    p2s = [mm4(p1, p1) for p1 in p1s]
    chain_slot()
    xs = [eye4 + p1 for p1 in p1s]
    xs = [x + mm4(x, p2) for x, p2 in zip(xs, p2s)]
    chain_slot()
    p4s = [mm4(p2, p2) for p2 in p2s]
    chain_slot()
    xs = [x + mm4(x, p4) for x, p4 in zip(xs, p4s)]
    chain_slot()
    for inner, outer in ((m8, m16), (m16, m32), (m32, None)):
        sel = (1.0 - inner) if outer is None else (outer - inner)
        ys = [mm4(x, l4 * sel) for x, l4 in zip(xs, l4s)]
        chain_slot()
        xs = [x + mm4(y, x) for x, y in zip(xs, ys)]
        chain_slot()

    for i, ch in enumerate(chunks):
        t_inv = xs[i // 2][:, (i % 2) * LANES:(i % 2 + 1) * LANES]
        w_u0 = _mm(t_inv, hstack(split_heads(ch["a_t"]), split_heads(ch["lv"])), po)
        ch.update(w=w_u0[:, :LANES], u0=w_u0[:, LANES:])
    chain_slot()
    for ch in chunks:
        m = _mm(ch["w"].T, ch["b_t"], po)
        ch.update(a_mat=(eye128 + m * bd128) * ch["p_end"])
    chain_slot()
    for ch in chunks:
        b_raw = _mm(vstack(ch["u0"], ch["v"]).T, vstack(ch["b_t"], ch["k_t"]), po)
        ch.update(b_mat=b_raw * bd128 * ch["p_end"])
    while next(chain_iter, None) is not None:
        raise AssertionError("not enough stages to interleave the state chain")
    assert not pending

    for idx, ch in enumerate(chunks):
        for q, name in enumerate(("w", "r_t", "u0", "y0", "m_rb")):
            t64_ref[idx, q] = ch[name]
        t128_ref[idx, 0] = ch["a_mat"]
        t128_ref[idx, 1] = ch["b_mat"]
    state_ref[0] = states[0]
    state_ref[1] = states[1]

    @pl.when(it == last)
    def _():
        sf_ref[0, :, 0] = state_ref[...]


def _rwkv_scan(rkvk, ld, a, k_a, s0, tile):
    b, t, _ = rkvk.shape
    n_t = t // tile
    n_chunks = tile // SCAN_CHUNK
    tile_in = lambda i: jnp.minimum(i, n_t - 1)
    tile_out = lambda i: jnp.maximum(i - 1, 0)
    blk = lambda col0, rev, pick: pl.BlockSpec(
        (1, tile, LANES), (lambda bi, hp, i: (bi, n_t - 1 - pick(i), col0 + hp)) if rev else
        (lambda bi, hp, i: (bi, pick(i), col0 + hp)))
    state_spec = pl.BlockSpec((1, 2, 1, LANES, LANES), lambda bi, hp, i: (bi, 0, hp, 0, 0))
    in_specs, args = [], []
    for z in range(2):
        in_specs += [blk(c0, z, tile_in) for c0 in (0, 4, 8, 12, 4 * z, 4 * z)]
        args += [rkvk, rkvk, rkvk, rkvk, ld, a]
    return pl.pallas_call(
        functools.partial(_rwkv_scan_kernel, n_chunks=n_chunks),
        grid=(b, 4, n_t + 1),
        in_specs=in_specs + [pl.BlockSpec((1, LANES), lambda bi, hp, i: (0, hp)), state_spec],
        out_specs=[blk(0, 0, tile_out), blk(0, 1, tile_out), state_spec],
        out_shape=[
            jax.ShapeDtypeStruct((b, t, D_WIDTH), F32),
            jax.ShapeDtypeStruct((b, t, D_WIDTH), F32),
            jax.ShapeDtypeStruct((b, 2, 4, LANES, LANES), F32),
        ],
        scratch_shapes=[pltpu.VMEM((2, LANES, LANES), F32),
                        pltpu.VMEM((2 * n_chunks, 5, SCAN_CHUNK, LANES), F32),
                        pltpu.VMEM((2 * n_chunks, 2, LANES, LANES), F32)],
        compiler_params=_cparams(("parallel", "arbitrary", "arbitrary")),
        name="rwkv_scan",
    )(*args, k_a.reshape(1, D_WIDTH), s0)


def _rwkv_out_kernel(yf_ref, yr_ref, rkvk_ref, a_ref, g_ref, ka_ref, rk_ref, lnw_ref, lnb_ref, bd_ref, o_ref):
    w = D_WIDTH
    bd = bd_ref[...]
    for j in range(w // LANES):
        cs = slice(j * LANES, (j + 1) * LANES)
        y = yf_ref[0, :, cs] + yr_ref[0, :, cs]
        mean = _head_sum(y, bd) * (1.0 / HEAD_DIM)
        yc = y - mean
        var = _head_sum(yc * yc, bd) * (1.0 / HEAD_DIM)
        yn = yc * lax.rsqrt(var + D_GN_EPS) * lnw_ref[:, cs] + lnb_ref[:, cs]
        r = rkvk_ref[0, :, cs]
        k = rkvk_ref[0, :, w + j * LANES:w + (j + 1) * LANES]
        v = rkvk_ref[0, :, 2 * w + j * LANES:2 * w + (j + 1) * LANES]
        bonus = jnp.zeros_like(y)
        for z in range(2):
            a = a_ref[0, :, z * w + j * LANES:z * w + (j + 1) * LANES]
            k_dir = k * (1.0 + (a - 1.0) * ka_ref[:, cs])
            bonus = bonus + _head_sum(r * k_dir * rk_ref[:, cs], bd) * v
        o_ref[0, :, cs] = ((yn + bonus) * g_ref[0, :, cs]).astype(o_ref.dtype)


def _rwkv_out(y_f, y_r, rkvk, a, g, k_a, r_k, ln_w, ln_b, tm):
    b, t, w = y_f.shape
    bd = jnp.asarray(np.kron(np.eye(2), np.ones((HEAD_DIM, HEAD_DIM))), BF16)
    const = lambda bi, i: (0, 0)
    vec = lambda p: p.reshape(1, w).astype(F32)
    return pl.pallas_call(
        _rwkv_out_kernel,
        grid=(b, t // tm),
        in_specs=[
            pl.BlockSpec((1, tm, w), lambda bi, i: (bi, i, 0)),
            pl.BlockSpec((1, tm, w), lambda bi, i: (bi, i, 0)),
            pl.BlockSpec((1, tm, 4 * w), lambda bi, i: (bi, i, 0)),
            pl.BlockSpec((1, tm, 2 * w), lambda bi, i: (bi, i, 0)),
            pl.BlockSpec((1, tm, w), lambda bi, i: (bi, i, 0)),
            pl.BlockSpec((1, w), const),
            pl.BlockSpec((1, w), const),
            pl.BlockSpec((1, w), const),
            pl.BlockSpec((1, w), const),
            pl.BlockSpec((LANES, LANES), const),
        ],
        out_specs=pl.BlockSpec((1, tm, w), lambda bi, i: (bi, i, 0)),
        out_shape=jax.ShapeDtypeStruct((b, t, w), BF16),
        compiler_params=_cparams(("parallel", "arbitrary")),
        name="rwkv_out",
    )(y_f, y_r, rkvk, a, g, vec(k_a), vec(r_k), vec(ln_w), vec(ln_b), bd)


def _mod_vectors(mods_layer, b):
    d = D_MODEL
    lat = [mods_layer[:b, j * d:(j + 1) * d].reshape(b, 1, d) for j in range(6)]
    ctx = [jnp.broadcast_to(mods_layer[b:b + 1, j * d:(j + 1) * d].reshape(1, 1, d), (b, 1, d)) for j in range(6)]
    return lat, ctx


def _tile2(v):
    return jnp.concatenate([v, v]).astype(F32)


def _ab_layer(x_lat, x_ctx, mods_layer, layer, norm_mix, norm_ffn, wg, wu, wd, w_in, w_out,
              a_qn, a_kn, a_lam, a_subln, b_qn, b_kn, b_sink, rope_lat, rope_ctx, need_ctx):
    b, s_len, d = x_lat.shape
    (sh_l, sc_l, g_l, fsh_l, fsc_l, fg_l), (sh_c, sc_c, g_c, fsh_c, fsc_c, fg_c) = _mod_vectors(mods_layer, b)
    lambda_init = 0.8 - 0.6 * math.exp(-0.3 * layer)

    aw = A_HEADS * LANES
    aq, ak, av = w_in[:, :aw], w_in[:, aw:2 * aw], w_in[:, 2 * aw:3 * aw]
    bq = w_in[:, 3 * aw:4 * aw]
    bk = w_in[:, 4 * aw:4 * aw + LANES]
    bv = w_in[:, 4 * aw + LANES:4 * aw + 2 * LANES]
    dup = lambda w: jnp.concatenate([w[:, :64], w[:, :64], w[:, 64:], w[:, 64:]], axis=1)
    w_cat = jnp.concatenate([aq, ak, bq, dup(bk), dup(bv), av], axis=1).astype(BF16)
    scale = HEAD_DIM ** -0.5
    hg = jnp.stack([_tile2(a_qn) * (scale * LOG2E), _tile2(a_kn), _tile2(b_qn) * scale, _tile2(b_kn)]
                   + [jnp.zeros((LANES,), F32)] * 4)
    plan = ([("qk", 0, 0, j) for j in range(4)] + [("qk", 1, 0, 4 + j) for j in range(4)]
            + [("qk", 2, 0, 8 + j) for j in range(4)] + [("qk", 3, 0, 12 + j) for j in range(2)]
            + [("v", 0, 0, 14 + j) for j in range(2)] + [("vT", 0, 1, j) for j in range(4)])
    out_defs = [("rows", 16, BF16), ("vT", 4, BF16)]
    qkv_l, avT_l = _inproj(x_lat, sh_l, sc_l, norm_mix, w_cat, rope_lat[0], rope_lat[1], hg, plan, out_defs, tm=512)
    qkv_c, avT_c = _inproj(x_ctx, sh_c, sc_c, norm_mix, w_cat, rope_ctx[0], rope_ctx[1], hg, plan, out_defs,
                           tm=x_ctx.shape[1])

    lam_f = a_lam.astype(F32)
    lam = (jnp.exp(jnp.sum(lam_f[0] * lam_f[1])) - jnp.exp(jnp.sum(lam_f[2] * lam_f[3])) + lambda_init).reshape(1, 1)
    post = 1.0 - lambda_init
    ak_all = jnp.concatenate([qkv_c[:, :, 4 * LANES:8 * LANES], qkv_l[:, :, 4 * LANES:8 * LANES]], axis=1)
    avT_all = jnp.concatenate([avT_c, avT_l], axis=-1)
    a_lat = _diffattn(lam, qkv_l, 0, ak_all, 0, avT_all, a_subln, post)
    b_lat = _window_attn(b_sink, qkv_l, 8, 12, 14, qkv_c, with_win=True)
    x_lat = _out_ffn(x_lat, a_lat, b_lat, w_out, g_l, norm_ffn, fsh_l, fsc_l, fg_l, wg, wu, wd, tm=FFN_ROW_TILE)
    if need_ctx:
        a_ctx = _diffattn(lam, qkv_c, 0, qkv_c, 4, avT_c, a_subln, post)
        b_ctx = _window_attn(b_sink, qkv_c, 8, 12, 14, qkv_c, with_win=False)
        x_ctx = _out_ffn(x_ctx, a_ctx, b_ctx, w_out, g_c, norm_ffn, fsh_c, fsc_c, fg_c, wg, wu, wd, tm=256)
    return x_lat, x_ctx


def _pad_block(m):
    return jnp.concatenate([m, jnp.zeros(m.shape[:-1] + (LANES - m.shape[-1],), m.dtype)], axis=-1)


def _cd_layer(x_lat, x_ctx, mods_layer, norm_mix, norm_ffn, wg, wu, wd, w_in, w_out, c_qn, c_kn, c_rpb,
              d_mu, d_w0, d_w2, d_a0, d_a2, d_g2, d_k_k, d_k_a, d_r_k, d_ln_w, d_ln_b):
    b, s_len, d = x_lat.shape
    n_ctx = x_ctx.shape[1]
    (sh_l, sc_l, g_l, fsh_l, fsc_l, fg_l), (sh_c, sc_c, _, _, _, _) = _mod_vectors(mods_layer, b)
    w = D_WIDTH
    lora = lambda m: [_pad_block(m[..., 3 * w:3 * w + 64]), _pad_block(m[..., 3 * w + 64:3 * w + 128]),
                      _pad_block(m[..., 3 * w + 128:])]
    w_d = w_in[:, 3 * w:]
    w_cat = jnp.concatenate([w_in[:, :3 * w], w_d[:, :3 * w]] + lora(w_d), axis=1).astype(BF16)
    mu_p = jnp.concatenate([d_mu[:3 * w]] + lora(d_mu))
    scale = HEAD_DIM ** -0.5
    hg = jnp.stack([_tile2(c_qn) * scale, _tile2(c_kn)] + [jnp.zeros((LANES,), F32)] * 6)
    plan = ([("qk", 0, 0, j) for j in range(4)] + [("qk", 1, 0, 4 + j) for j in range(4)]
            + [("v", 0, 0, 8 + j) for j in range(4)] + [("raw", 0, 1, j) for j in range(15)])
    out_defs = [("rows", 12, BF16), ("rows", 15, F32)]
    no_rope = lambda n: (jnp.ones((n, LANES), F32), jnp.zeros((n, LANES), F32))
    qkv_l, dcols_l = _inproj(x_lat, sh_l, sc_l, norm_mix, w_cat, *no_rope(s_len), hg, plan, out_defs, tm=512)
    qkv_c, dcols_c = _inproj(x_ctx, sh_c, sc_c, norm_mix, w_cat, *no_rope(n_ctx), hg, plan, out_defs, tm=n_ctx)

    c_lat = _natten(qkv_l, qkv_c, _natten_bias(c_rpb))

    prep = functools.partial(_rwkv_prep, mu=mu_p, w0=d_w0, w2=d_w2, a0=d_a0, a2=d_a2, g2=d_g2, k_k=d_k_k, tm=256)
    rkvk_c, ld_c, a_c, _ = prep(dcols_c)
    rkvk_l, ld_l, a_l, g_l_gate = prep(dcols_l)
    s0 = jnp.zeros((b, 2, 4, LANES, LANES), F32)
    _, _, s_ctx = _rwkv_scan(rkvk_c, ld_c, a_c, d_k_a, s0, tile=min(SCAN_TILE, n_ctx))
    y_f, y_r, _ = _rwkv_scan(rkvk_l, ld_l, a_l, d_k_a, s_ctx, tile=SCAN_TILE)
    d_lat = _rwkv_out(y_f, y_r, rkvk_l, a_l, g_l_gate, d_k_a, d_r_k, d_ln_w, d_ln_b, tm=512)
    return _out_ffn(x_lat, c_lat, d_lat, w_out, g_l, norm_ffn, fsh_l, fsc_l, fg_l, wg, wu, wd, tm=FFN_ROW_TILE)


def kernel(x, c, ctx, c_ctx, ada_w, ada_b, norm_mix, norm_ffn, ffn_w_gate, ffn_w_up, ffn_w_down, ab_w_in, ab_w_out, a_q_norm, a_k_norm, a_lambda, a_subln, b_q_norm, b_k_norm, b_sink, cd_w_in, cd_w_out, c_q_norm, c_k_norm, c_rpb, d_mu, d_w0, d_w2, d_a0, d_a2, d_g2, d_k_k, d_k_a, d_r_k, d_ln_w, d_ln_b):
    b, s_len, d = x.shape
    n_ctx = ctx.shape[1]
    cond = jnp.concatenate([c, c_ctx[None, :], jnp.zeros((8 - b - 1, d), F32)], axis=0)
    mods = _adaln(cond, ada_w, ada_b)
    rope_lat = _rope_tables(s_len)
    rope_ctx = (jnp.ones((n_ctx, LANES), F32), jnp.zeros((n_ctx, LANES), F32))
    x_lat, x_ctx = x, ctx
    x_lat, x_ctx = _ab_layer(
        x_lat, x_ctx, mods[0], 0, norm_mix[0], norm_ffn[0], ffn_w_gate[0].astype(BF16), ffn_w_up[0].astype(BF16),
        ffn_w_down[0].astype(BF16), ab_w_in[0], ab_w_out[0].astype(BF16), a_q_norm[0], a_k_norm[0], a_lambda[0],
        a_subln[0], b_q_norm[0], b_k_norm[0], b_sink[0], rope_lat, rope_ctx, need_ctx=True)
    return _cd_layer(
        x_lat, x_ctx, mods[1], norm_mix[1], norm_ffn[1], ffn_w_gate[1].astype(BF16), ffn_w_up[1].astype(BF16),
        ffn_w_down[1].astype(BF16), cd_w_in[0], cd_w_out[0].astype(BF16), c_q_norm[0], c_k_norm[0], c_rpb[0],
        d_mu[0], d_w0[0], d_w2[0], d_a0[0], d_a2[0], d_g2[0], d_k_k[0], d_k_a[0], d_r_k[0], d_ln_w[0], d_ln_b[0])
```

```python
import functools
import math

import jax
import jax.numpy as jnp
import numpy as np
from jax import lax
from jax.experimental import pallas as pl
from jax.experimental.pallas import tpu as pltpu

F32 = jnp.float32
BF16 = jnp.bfloat16

D_MODEL = 1024
GRID_W = 64
HEAD_DIM = 64
ROPE_THETA = 10000.0
NORM_EPS = 1e-6
NEG_INF = -1e30
LANES = 128
A_HEADS = 4
WINDOW = 128
NA_ROWS = 8
NA_COLS = 16
D_HEADS = 8
D_WIDTH = 512
D_GN_EPS = 64e-5
LOG2E = math.log2(math.e)
VMEM_LIMIT = 56 * 1024 * 1024


def _cparams(sem):
    return pltpu.CompilerParams(dimension_semantics=sem, vmem_limit_bytes=VMEM_LIMIT)


def _dot(a, b):
    return jnp.dot(a, b, preferred_element_type=F32)


def _dot_nt(a, b):
    return lax.dot_general(a, b, (((1,), (1,)), ((), ())), preferred_element_type=F32)


def _split(x):
    hi = x.astype(BF16)
    lo = (x - hi.astype(F32)).astype(BF16)
    return hi, lo


def _dot3(a, b):
    ah, al = _split(a)
    bh, bl = _split(b)
    return _dot(ah, bh) + _dot(al, bh) + _dot(ah, bl)


def _dot3_nt(a, b):
    ah, al = _split(a)
    bh, bl = _split(b)
    return _dot_nt(ah, bh) + _dot_nt(al, bh) + _dot_nt(ah, bl)


def _dot2_exact_rhs(a, b_bf16):
    ah, al = _split(a)
    return _dot(ah, b_bf16) + _dot(al, b_bf16)


def _norm_mod(x, gain, shift, scale):
    ms = jnp.mean(x * x, axis=-1, keepdims=True)
    y = x * lax.rsqrt(ms + NORM_EPS) * gain
    return y * (1.0 + scale) + shift


def _adaln_kernel(c_ref, w_ref, b_ref, o_ref):
    c = c_ref[...]
    o_ref[0] = _dot3(jax.nn.silu(c), w_ref[0]) + b_ref[0]


def _adaln(cond, ada_w, ada_b):
    depth, d, n = ada_w.shape
    tn = 1536
    return pl.pallas_call(
        _adaln_kernel,
        grid=(depth, n // tn),
        in_specs=[
            pl.BlockSpec((8, d), lambda l, j: (0, 0)),
            pl.BlockSpec((1, d, tn), lambda l, j: (l, 0, j)),
            pl.BlockSpec((1, 1, tn), lambda l, j: (l, 0, j)),
        ],
        out_specs=pl.BlockSpec((1, 8, tn), lambda l, j: (l, 0, j)),
        out_shape=jax.ShapeDtypeStruct((depth, 8, n), F32),
        compiler_params=_cparams(("arbitrary", "arbitrary")),
        name="adaln",
    )(cond, ada_w, ada_b.reshape(depth, 1, n))


def _head_norm_rope(y, bd, hg, cos, sin):
    ss = _dot2_exact_rhs(y * y, bd)
    y = y * lax.rsqrt(ss * (1.0 / HEAD_DIM) + NORM_EPS) * hg
    lane = lax.broadcasted_iota(jnp.int32, y.shape, 1)
    first_half = (lane % 32) < 16
    partner = jnp.where(first_half, pltpu.roll(y, LANES - 16, axis=1), pltpu.roll(y, 16, axis=1))
    return y * cos + partner * sin


def _inproj_kernel(x_ref, sh_ref, sc_ref, gain_ref, w_ref, cos_ref, sin_ref, hg_ref, bd_ref, *out_refs, plan):
    h = _norm_mod(x_ref[0], gain_ref[...], sh_ref[0], sc_ref[0]).astype(BF16)
    bd = bd_ref[...]
    cos = cos_ref[...]
    sin = sin_ref[...]
    n_blocks = len(plan)
    project = lambda j: _dot(h, w_ref[:, j * LANES:(j + min(2, n_blocks - j)) * LANES])
    y_next = project(0)
    for j0 in range(0, n_blocks, 2):
        width = min(2, n_blocks - j0) * LANES
        y2 = y_next
        if j0 + 2 < n_blocks:
            y_next = project(j0 + 2)
        for jj in range(width // LANES):
            kind, hg_idx, out_idx, out_blk = plan[j0 + jj]
            y = y2[:, jj * LANES:(jj + 1) * LANES]
            o_ref = out_refs[out_idx]
            if kind == "qk":
                y = _head_norm_rope(y, bd, hg_ref[hg_idx:hg_idx + 1, :], cos, sin)
                o_ref[0, :, out_blk * LANES:(out_blk + 1) * LANES] = y.astype(o_ref.dtype)
            elif kind == "vT":
                o_ref[0, out_blk] = y.T.astype(o_ref.dtype)
            else:
                o_ref[0, :, out_blk * LANES:(out_blk + 1) * LANES] = y.astype(o_ref.dtype)


def _inproj(x, shift, scale, gain, w, cos, sin, hg, plan, out_defs, tm):
    b, t, d = x.shape
    n = w.shape[1]
    bd = jnp.asarray(np.kron(np.eye(2), np.ones((HEAD_DIM, HEAD_DIM))), BF16)
    out_shapes, out_specs = [], []
    for kind, nblk, dt in out_defs:
        if kind == "rows":
            out_shapes.append(jax.ShapeDtypeStruct((b, t, nblk * LANES), dt))
            out_specs.append(pl.BlockSpec((1, tm, nblk * LANES), lambda bi, i: (bi, i, 0)))
        else:
            out_shapes.append(jax.ShapeDtypeStruct((b, nblk, LANES, t), dt))
            out_specs.append(pl.BlockSpec((1, nblk, LANES, tm), lambda bi, i: (bi, 0, 0, i)))
    return pl.pallas_call(
        functools.partial(_inproj_kernel, plan=tuple(plan)),
        grid=(b, t // tm),
        in_specs=[
            pl.BlockSpec((1, tm, d), lambda bi, i: (bi, i, 0)),
            pl.BlockSpec((1, 1, d), lambda bi, i: (bi, 0, 0)),
            pl.BlockSpec((1, 1, d), lambda bi, i: (bi, 0, 0)),
            pl.BlockSpec((1, d), lambda bi, i: (0, 0)),
            pl.BlockSpec((d, n), lambda bi, i: (0, 0)),
            pl.BlockSpec((tm, LANES), lambda bi, i: (i, 0)),
            pl.BlockSpec((tm, LANES), lambda bi, i: (i, 0)),
            pl.BlockSpec(hg.shape, lambda bi, i: (0, 0)),
            pl.BlockSpec((LANES, LANES), lambda bi, i: (0, 0)),
        ],
        out_specs=out_specs,
        out_shape=out_shapes,
        compiler_params=_cparams(("parallel", "arbitrary")),
        name="inproj",
    )(x, shift, scale, gain.reshape(1, d), w, cos, sin, hg, bd)


def _rope_tables(n_tokens):
    axis_dim = HEAD_DIM // 2
    freqs = ROPE_THETA ** (-jnp.arange(0, axis_dim, 2, dtype=F32) / axis_dim)
    t = jnp.arange(n_tokens, dtype=jnp.int32)
    lane = np.arange(LANES)
    d = lane % HEAD_DIM
    use_col = (d // 32) == 1
    f_idx = (d % 32) % 16
    sign = np.where((d % 32) < 16, -1.0, 1.0).astype(np.float32)
    pos = jnp.where(use_col[None, :], (t % GRID_W)[:, None], (t // GRID_W)[:, None]).astype(F32)
    ang = pos * freqs[f_idx][None, :]
    return jnp.cos(ang), jnp.sin(ang) * sign[None, :]


def _diffattn_kernel(lam_ref, q_ref, k_ref, vT_ref, subln_ref, o_ref, *s_refs, tq, tk, n_tiles, post_scale):
    n_q = q_ref.shape[1] // tq
    lane = lax.broadcasted_iota(jnp.int32, (tq, LANES), 1)
    sub = DIFF_SUB_BLOCK
    n_sub = tk // sub
    n_buf = len(s_refs)
    ahead = n_buf - 1

    def query_weights(qt):
        q = q_ref[0, qt * tq:(qt + 1) * tq, :]
        zero = jnp.zeros_like(q)
        qq = jnp.concatenate([jnp.where(lane < HEAD_DIM, q, zero), jnp.where(lane >= HEAD_DIM, q, zero)], axis=0)
        return qq.astype(F32).T.astype(BF16)

    qq_t = [query_weights(qt) for qt in range(n_q)]

    def scores(qt, i, s_ref):
        s = _dot(k_ref[0, i * tk:(i + 1) * tk, :], qq_t[qt])
        s_ref[...] = s
        return jnp.max(s, axis=0, keepdims=True)

    def finish(qt, l, acc):
        o = acc / l
        oT = o[:, :tq] - lam_ref[0, 0] * o[:, tq:]
        ms = jnp.mean(oT * oT, axis=0, keepdims=True)
        oT = oT * lax.rsqrt(ms + NORM_EPS) * subln_ref[...] * post_scale
        o_ref[0, qt * tq:(qt + 1) * tq, :] = oT.T.astype(o_ref.dtype)

    items = [(qt, i) for qt in range(n_q) for i in range(n_tiles)]
    cmaxes = {}
    for n in range(min(ahead, len(items))):
        cmaxes[n] = scores(*items[n], s_refs[n % n_buf])
    state = None
    for n, (qt, i) in enumerate(items):
        if i == 0:
            state = (jnp.full((1, 2 * tq), -jnp.inf, F32), jnp.zeros((1, 2 * tq), F32),
                     jnp.zeros((LANES, 2 * tq), F32))
        m, l, acc = state
        m_new = jnp.maximum(m, cmaxes.pop(n))
        alpha = jnp.exp2(m - m_new)
        s_ref = s_refs[n % n_buf]
        if n + ahead < len(items):
            cmaxes[n + ahead] = scores(*items[n + ahead], s_refs[(n + ahead) % n_buf])
        pv, psum = None, None
        for j in range(n_sub):
            p = jnp.exp2(s_ref[j * sub:(j + 1) * sub, :] - m_new)
            ps = jnp.sum(p, axis=0, keepdims=True)
            d = _dot(vT_ref[0, 0, :, i * tk + j * sub:i * tk + (j + 1) * sub], p.astype(BF16))
            psum = ps if psum is None else psum + ps
            pv = d if pv is None else pv + d
        state = (m_new, alpha * l + psum, alpha * acc + pv)
        if i == n_tiles - 1:
            finish(qt, state[1], state[2])


DIFF_KEY_TILE = 1280
DIFF_SUB_BLOCK = 256
DIFF_LOOKAHEAD = 2
DIFF_Q_TILES_PER_STEP = 2


def _diffattn(lam, q_arr, q_blk0, k_arr, k_blk0, vT, subln, post_scale, tq=128):
    b, t_q, _ = q_arr.shape
    t_k = k_arr.shape[1]
    tk = DIFF_KEY_TILE if (t_k % DIFF_KEY_TILE == 0 and t_k > DIFF_KEY_TILE) else 256
    rows = tq * DIFF_Q_TILES_PER_STEP
    return pl.pallas_call(
        functools.partial(_diffattn_kernel, tq=tq, tk=tk, n_tiles=t_k // tk, post_scale=post_scale),
        grid=(b, A_HEADS, t_q // rows),
        in_specs=[
            pl.BlockSpec(memory_space=pltpu.SMEM),
            pl.BlockSpec((1, rows, LANES), lambda bi, h, i: (bi, i, q_blk0 + h)),
            pl.BlockSpec((1, t_k, LANES), lambda bi, h, i: (bi, 0, k_blk0 + h)),
            pl.BlockSpec((1, 1, LANES, t_k), lambda bi, h, i: (bi, h, 0, 0)),
            pl.BlockSpec((LANES, 1), lambda bi, h, i: (0, 0)),
        ],
        out_specs=pl.BlockSpec((1, rows, LANES), lambda bi, h, i: (bi, i, h)),
        out_shape=jax.ShapeDtypeStruct((b, t_q, A_HEADS * LANES), BF16),
        scratch_shapes=[pltpu.VMEM((tk, 2 * tq), F32)] * (DIFF_LOOKAHEAD + 1),
        compiler_params=_cparams(("parallel", "arbitrary", "arbitrary")),
        name="diffattn",
    )(lam, q_arr, k_arr, vT, subln.reshape(LANES, 1))


WINDOW_BLOCKS_PER_STEP = 4

def _window_kernel(sink_ref, q_ref, kc_ref, vc_ref, *rest, with_win, n_blocks):
    if with_win:
        kp_ref, k0_ref, kn_ref, vp_ref, v0_ref, vn_ref, o_ref = rest
    else:
        (o_ref,) = rest
    i = pl.program_id(1)
    tq = WINDOW
    n_sb = q_ref.shape[1] // tq
    n_ctx = kc_ref.shape[1]
    lane = lax.broadcasted_iota(jnp.int32, (1, LANES), 1)
    lo = lane < HEAD_DIM
    if with_win:
        r = lax.broadcasted_iota(jnp.int32, (tq, tq), 0)
        c = lax.broadcasted_iota(jnp.int32, (tq, tq), 1)
    problems = [(sb, g) for sb in range(n_sb) for g in range(2)]
    scores, vcats, valids = [], [], []
    for sb, g in problems:
        gs = slice(g * LANES, (g + 1) * LANES)
        qs = slice(sb * tq, (sb + 1) * tq)
        if with_win:
            k_blocks = [kp_ref[0, :, gs]] + [k0_ref[0, t * tq:(t + 1) * tq, gs] for t in range(n_sb)] + [kn_ref[0, :, gs]]
            v_blocks = [vp_ref[0, :, gs]] + [v0_ref[0, t * tq:(t + 1) * tq, gs] for t in range(n_sb)] + [vn_ref[0, :, gs]]
            kcat = jnp.concatenate(k_blocks[sb:sb + 3] + [kc_ref[0, :, gs]], axis=0)
            vcat = jnp.concatenate(v_blocks[sb:sb + 3] + [vc_ref[0, :, gs]], axis=0)
            blk = i * n_sb + sb
            ok_prev = jnp.logical_and(c >= r, blk > 0)
            ok_next = jnp.logical_and(c <= r, blk < n_blocks - 1)
            valids.append(jnp.concatenate([ok_prev, jnp.ones((tq, tq), jnp.bool_), ok_next,
                                           jnp.ones((tq, n_ctx), jnp.bool_)], axis=1))
        else:
            kcat = kc_ref[0, :, gs]
            vcat = vc_ref[0, :, gs]
        q_rows = []
        for cb in range(2):
            qb = q_ref[0, qs, (g * 2 + cb) * LANES:(g * 2 + cb + 1) * LANES]
            zq = jnp.zeros_like(qb)
            q_rows += [jnp.where(lo, qb, zq), jnp.where(lo, zq, qb)]
        scores.append(_dot_nt(jnp.concatenate(q_rows, axis=0), kcat))
        vcats.append(vcat)
    for n, (sb, g) in enumerate(problems):
        s = scores[n]
        if with_win:
            s = jnp.where(jnp.concatenate([valids[n]] * 4, axis=0), s, NEG_INF)
        snk = jnp.concatenate([jnp.broadcast_to(sink_ref[g * 4 + h:g * 4 + h + 1, 0:1], (tq, 1)) for h in range(4)],
                              axis=0)
        m = jnp.maximum(jnp.max(s, axis=-1, keepdims=True), snk)
        e = jnp.exp(s - m)
        denom = jnp.sum(e, axis=-1, keepdims=True) + jnp.exp(snk - m)
        o = _dot(e.astype(BF16), vcats[n]) / denom
        for cb in range(2):
            out = jnp.where(lo, o[(2 * cb) * tq:(2 * cb + 1) * tq], o[(2 * cb + 1) * tq:(2 * cb + 2) * tq])
            o_ref[0, sb * tq:(sb + 1) * tq, (g * 2 + cb) * LANES:(g * 2 + cb + 1) * LANES] = out.astype(o_ref.dtype)


def _window_attn(sink, q_arr, q_blk0, k_blk0, v_blk0, ctx_arr, with_win):
    b, t_q, _ = q_arr.shape
    n_ctx = ctx_arr.shape[1]
    n_sb = min(WINDOW_BLOCKS_PER_STEP, t_q // WINDOW)
    rows = n_sb * WINDOW
    nb = t_q // WINDOW
    sink_tab = jnp.broadcast_to(sink.astype(F32).reshape(8, 1), (8, LANES))
    in_specs = [
        pl.BlockSpec((8, LANES), lambda bi, i: (0, 0)),
        pl.BlockSpec((1, rows, 4 * LANES), lambda bi, i: (bi, i, q_blk0 // 4)),
        pl.BlockSpec((1, n_ctx, 2 * LANES), lambda bi, i: (bi, 0, k_blk0 // 2)),
        pl.BlockSpec((1, n_ctx, 2 * LANES), lambda bi, i: (bi, 0, v_blk0 // 2)),
    ]
    args = [sink_tab, q_arr, ctx_arr, ctx_arr]
    if with_win:
        for blk0 in (k_blk0, v_blk0):
            in_specs += [
                pl.BlockSpec((1, WINDOW, 2 * LANES), lambda bi, i, c=blk0 // 2: (bi, jnp.maximum(i * n_sb - 1, 0), c)),
                pl.BlockSpec((1, rows, 2 * LANES), lambda bi, i, c=blk0 // 2: (bi, i, c)),
                pl.BlockSpec((1, WINDOW, 2 * LANES),
                             lambda bi, i, c=blk0 // 2: (bi, jnp.minimum((i + 1) * n_sb, nb - 1), c)),
            ]
            args += [q_arr, q_arr, q_arr]
    return pl.pallas_call(
        functools.partial(_window_kernel, with_win=with_win, n_blocks=nb),
        grid=(b, t_q // rows),
        in_specs=in_specs,
        out_specs=pl.BlockSpec((1, rows, 4 * LANES), lambda bi, i: (bi, i, 0)),
        out_shape=jax.ShapeDtypeStruct((b, t_q, 4 * LANES), BF16),
        compiler_params=_cparams(("parallel", "arbitrary")),
        name="window_attn",
    )(*args)


FFN_ROW_TILE = 512


def _out_ffn_kernel(x_ref, a_ref, b_ref, woa_ref, wob_ref, gate_ref, ng_ref, fsh_ref, fsc_ref, fg_ref,
                    wg_ref, wu_ref, wd_ref, o_ref):
    o = _dot(a_ref[0], woa_ref[...]) + _dot(b_ref[0], wob_ref[...])
    x1 = x_ref[0] + gate_ref[0] * o
    h = _norm_mod(x1, ng_ref[...], fsh_ref[0], fsc_ref[0]).astype(BF16)
    act = (jax.nn.silu(_dot(h, wg_ref[...])) * _dot(h, wu_ref[...])).astype(BF16)
    o_ref[0] = x1 + fg_ref[0] * _dot(act, wd_ref[...])


def _out_ffn(x, mix_a, mix_b, wo, gate, ng, fsh, fsc, fg, wg, wu, wd, tm):
    b, t, d = x.shape
    na, nb_ = mix_a.shape[-1], mix_b.shape[-1]
    dff = wg.shape[1]
    row = lambda bi, i: (bi, i, 0)
    vec = lambda bi, i: (bi, 0, 0)
    const = lambda bi, i: (0, 0)
    resident = functools.partial(pl.BlockSpec, index_map=const, pipeline_mode=pl.Buffered(1))
    return pl.pallas_call(
        _out_ffn_kernel,
        grid=(b, t // tm),
        in_specs=[
            pl.BlockSpec((1, tm, d), row),
            pl.BlockSpec((1, tm, na), row),
            pl.BlockSpec((1, tm, nb_), row),
            resident((na, d)),
            resident((nb_, d)),
            pl.BlockSpec((1, 1, d), vec),
            resident((1, d)),
            pl.BlockSpec((1, 1, d), vec),
            pl.BlockSpec((1, 1, d), vec),
            pl.BlockSpec((1, 1, d), vec),
            resident((d, dff)),
            resident((d, dff)),
            resident((dff, d)),
        ],
        out_specs=pl.BlockSpec((1, tm, d), row),
        out_shape=jax.ShapeDtypeStruct((b, t, d), F32),
        compiler_params=_cparams(("parallel", "arbitrary")),
        name="out_ffn",
    )(x, mix_a, mix_b, wo[:na], wo[na:], gate, ng.reshape(1, d), fsh, fsc, fg, wg, wu, wd)


NA_ROWS_PER_STEP = 16


def _natten_kernel(q_ref, k_ref, v_ref, kc_ref, vc_ref, bias_ref, o_ref, *, n_rows):
    step = pl.program_id(2)
    n_win = NA_ROWS * GRID_W
    lane = lax.broadcasted_iota(jnp.int32, (1, LANES), 1)
    lo = lane < HEAD_DIM
    keeps = (lo, jnp.logical_not(lo))
    q_all = q_ref[0]
    zq = jnp.zeros_like(q_all)
    q_half = [jnp.where(keep, q_all, zq) for keep in keeps]
    s_ctx = [_dot_nt(qh, kc_ref[0]) for qh in q_half]
    row_info, s_win = [], []
    for rr in range(NA_ROWS_PER_STEP):
        i = step * NA_ROWS_PER_STEP + rr
        r0 = jnp.clip(i - NA_ROWS // 2, 0, n_rows - NA_ROWS)
        koff = pl.multiple_of(r0 * GRID_W, GRID_W)
        row_info.append((i - r0, koff))
        k_win = k_ref[0, pl.ds(koff, n_win), :]
        for half in range(2):
            s_win.append(_dot_nt(q_half[half][rr * GRID_W:(rr + 1) * GRID_W], k_win))
    e_win, e_ctx, denom = [], [[], []], [[], []]
    for rr in range(NA_ROWS_PER_STEP):
        qs = slice(rr * GRID_W, (rr + 1) * GRID_W)
        for half in range(2):
            sw = s_win[2 * rr + half] + bias_ref[row_info[rr][0], half]
            sc = s_ctx[half][qs]
            m = jnp.maximum(jnp.max(sw, axis=-1, keepdims=True), jnp.max(sc, axis=-1, keepdims=True))
            ew, ec = jnp.exp(sw - m), jnp.exp(sc - m)
            e_win.append(ew.astype(BF16))
            e_ctx[half].append(ec.astype(BF16))
            denom[half].append(jnp.sum(ew, axis=-1, keepdims=True) + jnp.sum(ec, axis=-1, keepdims=True))
    vc = vc_ref[0]
    zc = jnp.zeros_like(vc)
    o_ctx = [_dot(jnp.concatenate(e_ctx[half], axis=0), jnp.where(keeps[half], vc, zc)) for half in range(2)]
    for rr in range(NA_ROWS_PER_STEP):
        qs = slice(rr * GRID_W, (rr + 1) * GRID_W)
        v_win = v_ref[0, pl.ds(row_info[rr][1], n_win), :]
        zv = jnp.zeros_like(v_win)
        out = None
        for half in range(2):
            o = (_dot(e_win[2 * rr + half], jnp.where(keeps[half], v_win, zv)) + o_ctx[half][qs]) / denom[half][rr]
            out = o if out is None else out + o
        o_ref[0, qs, :] = out.astype(o_ref.dtype)


def _natten_bias(rpb):
    n_heads = rpb.shape[0]
    qj = np.arange(GRID_W)[:, None]
    kj = np.arange(GRID_W)[None, :]
    cstart = np.clip(qj - NA_COLS // 2, 0, GRID_W - NA_COLS)
    in_win = (kj >= cstart) & (kj < cstart + NA_COLS)
    pad = GRID_W - NA_COLS
    rpb_p = jnp.pad(rpb.astype(F32), ((0, 0), (0, 0), (pad, pad)))
    toep = jnp.stack([rpb_p[:, :, pad + NA_COLS - 1 - q:pad + NA_COLS - 1 - q + GRID_W] for q in range(GRID_W)],
                     axis=2)
    toep = jnp.where(in_win[None, None], toep, NEG_INF)
    per_class = [jnp.moveaxis(toep[:, NA_ROWS - 1 - d:2 * NA_ROWS - 1 - d], 1, 2) for d in range(NA_ROWS)]
    return jnp.stack(per_class).reshape(NA_ROWS, n_heads, GRID_W, NA_ROWS * GRID_W)


def _natten(qkv, qkv_ctx, bias):
    b, s_len, _ = qkv.shape
    n_ctx = qkv_ctx.shape[1]
    n_rows = s_len // GRID_W
    tq = NA_ROWS_PER_STEP * GRID_W
    return pl.pallas_call(
        functools.partial(_natten_kernel, n_rows=n_rows),
        grid=(b, 4, n_rows // NA_ROWS_PER_STEP),
        in_specs=[
            pl.BlockSpec((1, tq, LANES), lambda bi, hp, i: (bi, i, hp)),
            pl.BlockSpec((1, s_len, LANES), lambda bi, hp, i: (bi, 0, 4 + hp)),
            pl.BlockSpec((1, s_len, LANES), lambda bi, hp, i: (bi, 0, 8 + hp)),
            pl.BlockSpec((1, n_ctx, LANES), lambda bi, hp, i: (bi, 0, 4 + hp)),
            pl.BlockSpec((1, n_ctx, LANES), lambda bi, hp, i: (bi, 0, 8 + hp)),
            pl.BlockSpec((NA_ROWS, 2, GRID_W, NA_ROWS * GRID_W), lambda bi, hp, i: (0, hp, 0, 0)),
        ],
        out_specs=pl.BlockSpec((1, tq, LANES), lambda bi, hp, i: (bi, i, hp)),
        out_shape=jax.ShapeDtypeStruct((b, s_len, 4 * LANES), BF16),
        compiler_params=_cparams(("parallel", "arbitrary", "arbitrary")),
        name="natten",
    )(qkv, qkv, qkv, qkv_ctx, qkv_ctx, bias)


def _head_sum(x, bd):
    return _dot2_exact_rhs(x, bd)


def _rwkv_prep_kernel(x_ref, xp_ref, xn_ref, mu_ref, w0_ref, w2_ref, a0_ref, a2_ref, g2_ref, kk_ref, bd_ref,
                      rkvk_ref, ld_ref, a_ref, g_ref):
    i = pl.program_id(1)
    n_t = pl.num_programs(1)
    x = x_ref[0]
    tm = x.shape[0]
    prev_row = jnp.where(i > 0, xp_ref[0, 7:8, :], 0.0)
    next_row = jnp.where(i < n_t - 1, xn_ref[0, 0:1, :], 0.0)
    rows = lax.broadcasted_iota(jnp.int32, (tm, 1), 0)
    x_prev = jnp.where(rows == 0, prev_row, pltpu.roll(x, 1, axis=0))
    x_next = jnp.where(rows == tm - 1, next_row, pltpu.roll(x, tm - 1, axis=0))
    xs = x + mu_ref[...] * (0.5 * (x_prev + x_next) - x)
    w = D_WIDTH
    k = xs[:, w:2 * w]
    rkvk_ref[0, :, 0:3 * w] = xs[:, 0:3 * w]
    kk = k * kk_ref[...]
    nrm = jnp.sqrt(jnp.concatenate([_head_sum(kk[:, j * LANES:(j + 1) * LANES] ** 2, bd_ref[...])
                                    for j in range(w // LANES)], axis=1))
    rkvk_ref[0, :, 3 * w:4 * w] = kk / jnp.maximum(nrm, 1e-12)
    wd = xs[:, 3 * w:3 * w + LANES]
    ad = xs[:, 3 * w + LANES:3 * w + 2 * LANES]
    gd = xs[:, 3 * w + 2 * LANES:3 * w + 3 * LANES]
    z = w0_ref[...] + _dot3(jnp.tanh(wd), w2_ref[...])
    ld_ref[0] = (-math.exp(-0.5)) * jax.nn.sigmoid(z)
    a_ref[0] = jax.nn.sigmoid(a0_ref[...] + _dot3(ad, a2_ref[...]))
    g_ref[0] = _dot3(jax.nn.sigmoid(gd), g2_ref[...])


def _rwkv_prep(dcols, mu, w0, w2, a0, a2, g2, k_k, tm):
    b, t, n = dcols.shape
    w = D_WIDTH
    bd = jnp.asarray(np.kron(np.eye(2), np.ones((HEAD_DIM, HEAD_DIM))), BF16)
    pad_rows = lambda m, r0: jnp.zeros((LANES, m.shape[1]), F32).at[r0:r0 + m.shape[0]].set(m)
    w2p = jnp.concatenate([pad_rows(w2[0], 0), pad_rows(w2[1], 32)], axis=1)
    a2p = jnp.concatenate([pad_rows(a2[0], 0), pad_rows(a2[1], 32)], axis=1)
    g2p = pad_rows(g2, 0)
    const = lambda bi, i: (0, 0)
    t8 = tm // 8
    return pl.pallas_call(
        _rwkv_prep_kernel,
        grid=(b, t // tm),
        in_specs=[
            pl.BlockSpec((1, tm, n), lambda bi, i: (bi, i, 0)),
            pl.BlockSpec((1, 8, n), lambda bi, i: (bi, jnp.maximum(i * t8 - 1, 0), 0)),
            pl.BlockSpec((1, 8, n), lambda bi, i: (bi, jnp.minimum((i + 1) * t8, t // 8 - 1), 0)),
            pl.BlockSpec((1, n), const),
            pl.BlockSpec((1, 2 * w), const),
            pl.BlockSpec((LANES, 2 * w), const),
            pl.BlockSpec((1, 2 * w), const),
            pl.BlockSpec((LANES, 2 * w), const),
            pl.BlockSpec((LANES, w), const),
            pl.BlockSpec((1, w), const),
            pl.BlockSpec((LANES, LANES), const),
        ],
        out_specs=[
            pl.BlockSpec((1, tm, 4 * w), lambda bi, i: (bi, i, 0)),
            pl.BlockSpec((1, tm, 2 * w), lambda bi, i: (bi, i, 0)),
            pl.BlockSpec((1, tm, 2 * w), lambda bi, i: (bi, i, 0)),
            pl.BlockSpec((1, tm, w), lambda bi, i: (bi, i, 0)),
        ],
        out_shape=[
            jax.ShapeDtypeStruct((b, t, 4 * w), F32),
            jax.ShapeDtypeStruct((b, t, 2 * w), F32),
            jax.ShapeDtypeStruct((b, t, 2 * w), F32),
            jax.ShapeDtypeStruct((b, t, w), F32),
        ],
        compiler_params=_cparams(("parallel", "arbitrary")),
        name="rwkv_prep",
    )(dcols, dcols, dcols, mu.reshape(1, n), w0.reshape(1, 2 * w), w2p, a0.reshape(1, 2 * w), a2p, g2p,
      k_k.reshape(1, w), bd)


SCAN_CHUNK = 64
SCAN_TILE = 512


def _split3(x):
    hi = x.astype(BF16)
    r1 = x - hi.astype(F32)
    mid = r1.astype(BF16)
    lo = (r1 - mid.astype(F32)).astype(BF16)
    return hi, mid, lo


def _mm(a, b, precise):
    return _dot3(a, b) if precise else _dot(a.astype(BF16), b.astype(BF16))


def _mm_nt(a, b, precise):
    return _dot3_nt(a, b) if precise else _dot_nt(a.astype(BF16), b.astype(BF16))


def _blockwise_mm(x, y, bd_mask, precise):
    n_blk = x.shape[1] // SCAN_CHUNK
    expand = lambda m: jnp.where(bd_mask, jnp.concatenate([m] * n_blk, axis=0), jnp.zeros((), m.dtype))
    if not precise:
        return _dot(x.astype(BF16), expand(y.astype(BF16)))
    xh, xl = _split(x)
    yh, yl = _split(y)
    ybd = expand(yh)
    return _dot(xh, ybd) + _dot(xl, ybd) + _dot(xh, expand(yl))


SCAN_PRECISE_INVERSE = False
SCAN_PRECISE_STATE = False
SCAN_PRECISE_OTHER = False


def _rwkv_scan_kernel(*refs, n_chunks):
    fwd_refs, rev_refs = refs[0:6], refs[6:12]
    ka_ref, s0_ref = refs[12:14]
    y_refs = refs[14:16]
    sf_ref, state_ref, t64_ref, t128_ref = refs[16:20]
    it = pl.program_id(2)
    last = pl.num_programs(2) - 1
    c = SCAN_CHUNK
    n_units = n_chunks // 2

    @pl.when(it == 0)
    def _():
        state_ref[...] = s0_ref[0, :, 0]
        t64_ref[...] = jnp.zeros_like(t64_ref)
        t128_ref[...] = jnp.zeros_like(t128_ref)

    ri = lax.broadcasted_iota(jnp.int32, (c, c), 0)
    ci = lax.broadcasted_iota(jnp.int32, (c, c), 1)
    eye = (ri == ci).astype(F32)
    tile_l = lambda m, n: jnp.concatenate([m] * n, axis=1)
    eye4 = tile_l(eye, 4)
    blk = lambda n: ((ri // n) == (ci // n)).astype(F32)
    m8, m16, m32 = tile_l(blk(8), 4), tile_l(blk(16), 4), tile_l(blk(32), 4)
    r4 = lax.broadcasted_iota(jnp.int32, (4 * c, 4 * c), 0)
    c4 = lax.broadcasted_iota(jnp.int32, (4 * c, 4 * c), 1)
    bd4 = (r4 // c) == (c4 // c)
    lane = lax.broadcasted_iota(jnp.int32, (1, LANES), 1)
    hm_lo = (lane < HEAD_DIM).astype(F32)
    hm_hi = 1.0 - hm_lo
    r2 = lax.broadcasted_iota(jnp.int32, (LANES, LANES), 0)
    c2 = lax.broadcasted_iota(jnp.int32, (LANES, LANES), 1)
    bd128 = ((r2 // HEAD_DIM) == (c2 // HEAD_DIM)).astype(F32)
    eye128 = (r2 == c2).astype(F32)
    ka = ka_ref[...]
    vstack = lambda *xs: jnp.concatenate(xs, axis=0)
    hstack = lambda *xs: jnp.concatenate(xs, axis=1)
    split_heads = lambda m: vstack(m * hm_lo, m * hm_hi)
    po = SCAN_PRECISE_OTHER

    rows = [slice(j * c, (j + 1) * c) for j in range(n_chunks)]
    masks = []
    for z in range(2):
        strict = ((ci > ri) if z == 1 else (ci < ri)).astype(F32)
        incl = strict + eye
        masks.append(dict(strict2=tile_l(strict, 2), incl2=tile_l(incl, 2), tri_b=incl.astype(BF16)))

    states = [state_ref[0], state_ref[1]]
    pending = {}

    def chain_step(step):
        for z in range(2):
            j = step if z == 0 else n_chunks - 1 - step
            idx = z * n_chunks + j
            s_in = states[z]
            s_new = _mm(s_in, t128_ref[idx, 0], SCAN_PRECISE_STATE) + t128_ref[idx, 1]
            states[z] = jnp.where(it > 0, s_new, s_in)
            wr = _mm_nt(vstack(t64_ref[idx, 0], t64_ref[idx, 1]), s_in, SCAN_PRECISE_STATE)
            pending[(step, z)] = (idx, j, wr)

    def chain_outputs(step):
        for z in range(2):
            idx, j, wr = pending.pop((step, z))
            u_full = wr[:c] + t64_ref[idx, 2]
            y_refs[z][0, rows[j], :] = wr[c:] + t64_ref[idx, 3] + _mm(t64_ref[idx, 4], split_heads(u_full), po)

    chain_iter = iter(range(n_chunks + 1))

    def chain_slot():
        k = next(chain_iter, None)
        if k is None:
            return
        if k > 0:
            chain_outputs(k - 1)
        if k < n_chunks:
            chain_step(k)

    cums = []
    for z, in_refs in enumerate((fwd_refs, rev_refs)):
        h3 = _split3(hstack(*[in_refs[4][0, rw, :] for rw in rows]))
        tri_b = masks[z]["tri_b"]
        cums.append(_dot(tri_b, h3[0]) + _dot(tri_b, h3[1]) + _dot(tri_b, h3[2]))
    chain_slot()

    chunks = []
    for z, in_refs in enumerate((fwd_refs, rev_refs)):
        r_ref, k_ref, v_ref, kk_ref, ld_ref, a_ref = in_refs
        for j, rw in enumerate(rows):
            r, k, v, kk = r_ref[0, rw, :], k_ref[0, rw, :], v_ref[0, rw, :], kk_ref[0, rw, :]
            ld, a = ld_ref[0, rw, :], a_ref[0, rw, :]
            cum = cums[z][:, j * LANES:(j + 1) * LANES]
            pinv = jnp.exp(-cum)
            p_end = jnp.exp(cum[0:1, :] if z == 1 else cum[c - 1:c, :])
            r_t = r * jnp.exp(cum)
            a_t = -kk * jnp.exp(cum - ld)
            b_t = kk * a * pinv
            k_t = k * (1.0 + (a - 1.0) * ka) * pinv
            chunks.append(dict(z=z, j=j, r_t=r_t, a_t=a_t, b_t=b_t, k_t=k_t, v=v, p_end=p_end))
    for ch in chunks:
        g = _mm_nt(vstack(ch["a_t"], ch["r_t"]), vstack(split_heads(ch["b_t"]), split_heads(ch["k_t"])), po)
        mk = masks[ch["z"]]
        ch.update(l_ab=g[:c, :2 * c] * mk["strict2"], l_ak=g[:c, 2 * c:] * mk["strict2"],
                  m_rb=g[c:, :2 * c] * mk["incl2"], m_rk=g[c:, 2 * c:] * mk["incl2"])
    chain_slot()
    for ch in chunks:
        lv_y0 = _mm(vstack(ch["l_ak"], ch["m_rk"]), split_heads(ch["v"]), po)
        ch.update(lv=lv_y0[:c], y0=lv_y0[c:])
    chain_slot()

    mm4 = lambda x, y: _blockwise_mm(x, y, bd4, SCAN_PRECISE_INVERSE)
    l4s = [hstack(chunks[2 * u]["l_ab"], chunks[2 * u + 1]["l_ab"]) for u in range(2 * n_units)]
    p1s = [l4 * m8 for l4 in l4s]
    p2s = [mm4(p1, p1) for p1 in p1s]
    chain_slot()
    xs = [eye4 + p1 for p1 in p1s]
    xs = [x + mm4(x, p2) for x, p2 in zip(xs, p2s)]
    chain_slot()
    p4s = [mm4(p2, p2) for p2 in p2s]
    chain_slot()
    xs = [x + mm4(x, p4) for x, p4 in zip(xs, p4s)]
    chain_slot()
    for inner, outer in ((m8, m16), (m16, m32), (m32, None)):
        sel = (1.0 - inner) if outer is None else (outer - inner)
        ys = [mm4(x, l4 * sel) for x, l4 in zip(xs, l4s)]
        chain_slot()
        xs = [x + mm4(y, x) for x, y in zip(xs, ys)]
        chain_slot()

    for i, ch in enumerate(chunks):
        t_inv = xs[i // 2][:, (i % 2) * LANES:(i % 2 + 1) * LANES]
        w_u0 = _mm(t_inv, hstack(split_heads(ch["a_t"]), split_heads(ch["lv"])), po)
        ch.update(w=w_u0[:, :LANES], u0=w_u0[:, LANES:])
    chain_slot()
    for ch in chunks:
        m = _mm(ch["w"].T, ch["b_t"], po)
        ch.update(a_mat=(eye128 + m * bd128) * ch["p_end"])
    chain_slot()
    for ch in chunks:
        b_raw = _mm(vstack(ch["u0"], ch["v"]).T, vstack(ch["b_t"], ch["k_t"]), po)
        ch.update(b_mat=b_raw * bd128 * ch["p_end"])
    while next(chain_iter, None) is not None:
        raise AssertionError("not enough stages to interleave the state chain")
    assert not pending

    for idx, ch in enumerate(chunks):
        for q, name in enumerate(("w", "r_t", "u0", "y0", "m_rb")):
            t64_ref[idx, q] = ch[name]
        t128_ref[idx, 0] = ch["a_mat"]
        t128_ref[idx, 1] = ch["b_mat"]
    state_ref[0] = states[0]
    state_ref[1] = states[1]

    @pl.when(it == last)
    def _():
        sf_ref[0, :, 0] = state_ref[...]


def _rwkv_scan(rkvk, ld, a, k_a, s0, tile):
    b, t, _ = rkvk.shape
    n_t = t // tile
    n_chunks = tile // SCAN_CHUNK
    tile_in = lambda i: jnp.minimum(i, n_t - 1)
    tile_out = lambda i: jnp.maximum(i - 1, 0)
    blk = lambda col0, rev, pick: pl.BlockSpec(
        (1, tile, LANES), (lambda bi, hp, i: (bi, n_t - 1 - pick(i), col0 + hp)) if rev else
        (lambda bi, hp, i: (bi, pick(i), col0 + hp)))
    state_spec = pl.BlockSpec((1, 2, 1, LANES, LANES), lambda bi, hp, i: (bi, 0, hp, 0, 0))
    in_specs, args = [], []
    for z in range(2):
        in_specs += [blk(c0, z, tile_in) for c0 in (0, 4, 8, 12, 4 * z, 4 * z)]
        args += [rkvk, rkvk, rkvk, rkvk, ld, a]
    return pl.pallas_call(
        functools.partial(_rwkv_scan_kernel, n_chunks=n_chunks),
        grid=(b, 4, n_t + 1),
        in_specs=in_specs + [pl.BlockSpec((1, LANES), lambda bi, hp, i: (0, hp)), state_spec],
        out_specs=[blk(0, 0, tile_out), blk(0, 1, tile_out), state_spec],
        out_shape=[
            jax.ShapeDtypeStruct((b, t, D_WIDTH), F32),
            jax.ShapeDtypeStruct((b, t, D_WIDTH), F32),
            jax.ShapeDtypeStruct((b, 2, 4, LANES, LANES), F32),
        ],
        scratch_shapes=[pltpu.VMEM((2, LANES, LANES), F32),
                        pltpu.VMEM((2 * n_chunks, 5, SCAN_CHUNK, LANES), F32),
                        pltpu.VMEM((2 * n_chunks, 2, LANES, LANES), F32)],
        compiler_params=_cparams(("parallel", "arbitrary", "arbitrary")),
        name="rwkv_scan",
    )(*args, k_a.reshape(1, D_WIDTH), s0)


def _rwkv_out_kernel(yf_ref, yr_ref, rkvk_ref, a_ref, g_ref, ka_ref, rk_ref, lnw_ref, lnb_ref, bd_ref, o_ref):
    w = D_WIDTH
    bd = bd_ref[...]
    for j in range(w // LANES):
        cs = slice(j * LANES, (j + 1) * LANES)
        y = yf_ref[0, :, cs] + yr_ref[0, :, cs]
        mean = _head_sum(y, bd) * (1.0 / HEAD_DIM)
        yc = y - mean
        var = _head_sum(yc * yc, bd) * (1.0 / HEAD_DIM)
        yn = yc * lax.rsqrt(var + D_GN_EPS) * lnw_ref[:, cs] + lnb_ref[:, cs]
        r = rkvk_ref[0, :, cs]
        k = rkvk_ref[0, :, w + j * LANES:w + (j + 1) * LANES]
        v = rkvk_ref[0, :, 2 * w + j * LANES:2 * w + (j + 1) * LANES]
        bonus = jnp.zeros_like(y)
        for z in range(2):
            a = a_ref[0, :, z * w + j * LANES:z * w + (j + 1) * LANES]
            k_dir = k * (1.0 + (a - 1.0) * ka_ref[:, cs])
            bonus = bonus + _head_sum(r * k_dir * rk_ref[:, cs], bd) * v
        o_ref[0, :, cs] = ((yn + bonus) * g_ref[0, :, cs]).astype(o_ref.dtype)


def _rwkv_out(y_f, y_r, rkvk, a, g, k_a, r_k, ln_w, ln_b, tm):
    b, t, w = y_f.shape
    bd = jnp.asarray(np.kron(np.eye(2), np.ones((HEAD_DIM, HEAD_DIM))), BF16)
    const = lambda bi, i: (0, 0)
    vec = lambda p: p.reshape(1, w).astype(F32)
    return pl.pallas_call(
        _rwkv_out_kernel,
        grid=(b, t // tm),
        in_specs=[
            pl.BlockSpec((1, tm, w), lambda bi, i: (bi, i, 0)),
            pl.BlockSpec((1, tm, w), lambda bi, i: (bi, i, 0)),
            pl.BlockSpec((1, tm, 4 * w), lambda bi, i: (bi, i, 0)),
            pl.BlockSpec((1, tm, 2 * w), lambda bi, i: (bi, i, 0)),
            pl.BlockSpec((1, tm, w), lambda bi, i: (bi, i, 0)),
            pl.BlockSpec((1, w), const),
            pl.BlockSpec((1, w), const),
            pl.BlockSpec((1, w), const),
            pl.BlockSpec((1, w), const),
            pl.BlockSpec((LANES, LANES), const),
        ],
        out_specs=pl.BlockSpec((1, tm, w), lambda bi, i: (bi, i, 0)),
        out_shape=jax.ShapeDtypeStruct((b, t, w), BF16),
        compiler_params=_cparams(("parallel", "arbitrary")),
        name="rwkv_out",
    )(y_f, y_r, rkvk, a, g, vec(k_a), vec(r_k), vec(ln_w), vec(ln_b), bd)


def _mod_vectors(mods_layer, b):
    d = D_MODEL
    lat = [mods_layer[:b, j * d:(j + 1) * d].reshape(b, 1, d) for j in range(6)]
    ctx = [jnp.broadcast_to(mods_layer[b:b + 1, j * d:(j + 1) * d].reshape(1, 1, d), (b, 1, d)) for j in range(6)]
    return lat, ctx


def _tile2(v):
    return jnp.concatenate([v, v]).astype(F32)


def _ab_layer(x_lat, x_ctx, mods_layer, layer, norm_mix, norm_ffn, wg, wu, wd, w_in, w_out,
              a_qn, a_kn, a_lam, a_subln, b_qn, b_kn, b_sink, rope_lat, rope_ctx, need_ctx):
    b, s_len, d = x_lat.shape
    (sh_l, sc_l, g_l, fsh_l, fsc_l, fg_l), (sh_c, sc_c, g_c, fsh_c, fsc_c, fg_c) = _mod_vectors(mods_layer, b)
    lambda_init = 0.8 - 0.6 * math.exp(-0.3 * layer)

    aw = A_HEADS * LANES
    aq, ak, av = w_in[:, :aw], w_in[:, aw:2 * aw], w_in[:, 2 * aw:3 * aw]
    bq = w_in[:, 3 * aw:4 * aw]
    bk = w_in[:, 4 * aw:4 * aw + LANES]
    bv = w_in[:, 4 * aw + LANES:4 * aw + 2 * LANES]
    dup = lambda w: jnp.concatenate([w[:, :64], w[:, :64], w[:, 64:], w[:, 64:]], axis=1)
    w_cat = jnp.concatenate([aq, ak, bq, dup(bk), dup(bv), av], axis=1).astype(BF16)
    scale = HEAD_DIM ** -0.5
    hg = jnp.stack([_tile2(a_qn) * (scale * LOG2E), _tile2(a_kn), _tile2(b_qn) * scale, _tile2(b_kn)]
                   + [jnp.zeros((LANES,), F32)] * 4)
    plan = ([("qk", 0, 0, j) for j in range(4)] + [("qk", 1, 0, 4 + j) for j in range(4)]
            + [("qk", 2, 0, 8 + j) for j in range(4)] + [("qk", 3, 0, 12 + j) for j in range(2)]
            + [("v", 0, 0, 14 + j) for j in range(2)] + [("vT", 0, 1, j) for j in range(4)])
    out_defs = [("rows", 16, BF16), ("vT", 4, BF16)]
    qkv_l, avT_l = _inproj(x_lat, sh_l, sc_l, norm_mix, w_cat, rope_lat[0], rope_lat[1], hg, plan, out_defs, tm=512)
    qkv_c, avT_c = _inproj(x_ctx, sh_c, sc_c, norm_mix, w_cat, rope_ctx[0], rope_ctx[1], hg, plan, out_defs,
                           tm=x_ctx.shape[1])

    lam_f = a_lam.astype(F32)
    lam = (jnp.exp(jnp.sum(lam_f[0] * lam_f[1])) - jnp.exp(jnp.sum(lam_f[2] * lam_f[3])) + lambda_init).reshape(1, 1)
    post = 1.0 - lambda_init
    ak_all = jnp.concatenate([qkv_c[:, :, 4 * LANES:8 * LANES], qkv_l[:, :, 4 * LANES:8 * LANES]], axis=1)
    avT_all = jnp.concatenate([avT_c, avT_l], axis=-1)
    a_lat = _diffattn(lam, qkv_l, 0, ak_all, 0, avT_all, a_subln, post)
    b_lat = _window_attn(b_sink, qkv_l, 8, 12, 14, qkv_c, with_win=True)
    x_lat = _out_ffn(x_lat, a_lat, b_lat, w_out, g_l, norm_ffn, fsh_l, fsc_l, fg_l, wg, wu, wd, tm=FFN_ROW_TILE)
    if need_ctx:
        a_ctx = _diffattn(lam, qkv_c, 0, qkv_c, 4, avT_c, a_subln, post)
        b_ctx = _window_attn(b_sink, qkv_c, 8, 12, 14, qkv_c, with_win=False)
        x_ctx = _out_ffn(x_ctx, a_ctx, b_ctx, w_out, g_c, norm_ffn, fsh_c, fsc_c, fg_c, wg, wu, wd, tm=256)
    return x_lat, x_ctx


def _pad_block(m):
    return jnp.concatenate([m, jnp.zeros(m.shape[:-1] + (LANES - m.shape[-1],), m.dtype)], axis=-1)


def _cd_layer(x_lat, x_ctx, mods_layer, norm_mix, norm_ffn, wg, wu, wd, w_in, w_out, c_qn, c_kn, c_rpb,
              d_mu, d_w0, d_w2, d_a0, d_a2, d_g2, d_k_k, d_k_a, d_r_k, d_ln_w, d_ln_b):
    b, s_len, d = x_lat.shape
    n_ctx = x_ctx.shape[1]
    (sh_l, sc_l, g_l, fsh_l, fsc_l, fg_l), (sh_c, sc_c, _, _, _, _) = _mod_vectors(mods_layer, b)
    w = D_WIDTH
    lora = lambda m: [_pad_block(m[..., 3 * w:3 * w + 64]), _pad_block(m[..., 3 * w + 64:3 * w + 128]),
                      _pad_block(m[..., 3 * w + 128:])]
    w_d = w_in[:, 3 * w:]
    w_cat = jnp.concatenate([w_in[:, :3 * w], w_d[:, :3 * w]] + lora(w_d), axis=1).astype(BF16)
    mu_p = jnp.concatenate([d_mu[:3 * w]] + lora(d_mu))
    scale = HEAD_DIM ** -0.5
    hg = jnp.stack([_tile2(c_qn) * scale, _tile2(c_kn)] + [jnp.zeros((LANES,), F32)] * 6)
    plan = ([("qk", 0, 0, j) for j in range(4)] + [("qk", 1, 0, 4 + j) for j in range(4)]
            + [("v", 0, 0, 8 + j) for j in range(4)] + [("raw", 0, 1, j) for j in range(15)])
    out_defs = [("rows", 12, BF16), ("rows", 15, F32)]
    no_rope = lambda n: (jnp.ones((n, LANES), F32), jnp.zeros((n, LANES), F32))
    qkv_l, dcols_l = _inproj(x_lat, sh_l, sc_l, norm_mix, w_cat, *no_rope(s_len), hg, plan, out_defs, tm=512)
    qkv_c, dcols_c = _inproj(x_ctx, sh_c, sc_c, norm_mix, w_cat, *no_rope(n_ctx), hg, plan, out_defs, tm=n_ctx)

    c_lat = _natten(qkv_l, qkv_c, _natten_bias(c_rpb))

    prep = functools.partial(_rwkv_prep, mu=mu_p, w0=d_w0, w2=d_w2, a0=d_a0, a2=d_a2, g2=d_g2, k_k=d_k_k, tm=256)
    rkvk_c, ld_c, a_c, _ = prep(dcols_c)
    rkvk_l, ld_l, a_l, g_l_gate = prep(dcols_l)
    s0 = jnp.zeros((b, 2, 4, LANES, LANES), F32)
    _, _, s_ctx = _rwkv_scan(rkvk_c, ld_c, a_c, d_k_a, s0, tile=min(SCAN_TILE, n_ctx))
    y_f, y_r, _ = _rwkv_scan(rkvk_l, ld_l, a_l, d_k_a, s_ctx, tile=SCAN_TILE)
    d_lat = _rwkv_out(y_f, y_r, rkvk_l, a_l, g_l_gate, d_k_a, d_r_k, d_ln_w, d_ln_b, tm=512)
    return _out_ffn(x_lat, c_lat, d_lat, w_out, g_l, norm_ffn, fsh_l, fsc_l, fg_l, wg, wu, wd, tm=FFN_ROW_TILE)


def kernel(x, c, ctx, c_ctx, ada_w, ada_b, norm_mix, norm_ffn, ffn_w_gate, ffn_w_up, ffn_w_down, ab_w_in, ab_w_out, a_q_norm, a_k_norm, a_lambda, a_subln, b_q_norm, b_k_norm, b_sink, cd_w_in, cd_w_out, c_q_norm, c_k_norm, c_rpb, d_mu, d_w0, d_w2, d_a0, d_a2, d_g2, d_k_k, d_k_a, d_r_k, d_ln_w, d_ln_b):
    b, s_len, d = x.shape
    n_ctx = ctx.shape[1]
    cond = jnp.concatenate([c, c_ctx[None, :], jnp.zeros((8 - b - 1, d), F32)], axis=0)
    mods = _adaln(cond, ada_w, ada_b)
    rope_lat = _rope_tables(s_len)
    rope_ctx = (jnp.ones((n_ctx, LANES), F32), jnp.zeros((n_ctx, LANES), F32))
    x_lat, x_ctx = x, ctx
    x_lat, x_ctx = _ab_layer(
        x_lat, x_ctx, mods[0], 0, norm_mix[0], norm_ffn[0], ffn_w_gate[0].astype(BF16), ffn_w_up[0].astype(BF16),
        ffn_w_down[0].astype(BF16), ab_w_in[0], ab_w_out[0].astype(BF16), a_q_norm[0], a_k_norm[0], a_lambda[0],
        a_subln[0], b_q_norm[0], b_k_norm[0], b_sink[0], rope_lat, rope_ctx, need_ctx=True)
    return _cd_layer(
        x_lat, x_ctx, mods[1], norm_mix[1], norm_ffn[1], ffn_w_gate[1].astype(BF16), ffn_w_up[1].astype(BF16),
        ffn_w_down[1].astype(BF16), cd_w_in[0], cd_w_out[0].astype(BF16), c_q_norm[0], c_k_norm[0], c_rpb[0],
        d_mu[0], d_w0[0], d_w2[0], d_a0[0], d_a2[0], d_g2[0], d_k_k[0], d_k_a[0], d_r_k[0], d_ln_w[0], d_ln_b[0])
```

```python
import functools
import math

import jax
import jax.numpy as jnp
import numpy as np
from jax import lax
from jax.experimental import pallas as pl
from jax.experimental.pallas import tpu as pltpu

F32 = jnp.float32
BF16 = jnp.bfloat16

D_MODEL = 1024
GRID_W = 64
HEAD_DIM = 64
ROPE_THETA = 10000.0
NORM_EPS = 1e-6
NEG_INF = -1e30
LANES = 128
A_HEADS = 4
WINDOW = 128
NA_ROWS = 8
NA_COLS = 16
D_HEADS = 8
D_WIDTH = 512
D_GN_EPS = 64e-5
LOG2E = math.log2(math.e)
VMEM_LIMIT = 56 * 1024 * 1024


def _cparams(sem):
    return pltpu.CompilerParams(dimension_semantics=sem, vmem_limit_bytes=VMEM_LIMIT)


def _dot(a, b):
    return jnp.dot(a, b, preferred_element_type=F32)


def _dot_nt(a, b):
    return lax.dot_general(a, b, (((1,), (1,)), ((), ())), preferred_element_type=F32)


def _split(x):
    hi = x.astype(BF16)
    lo = (x - hi.astype(F32)).astype(BF16)
    return hi, lo


def _dot3(a, b):
    ah, al = _split(a)
    bh, bl = _split(b)
    return _dot(ah, bh) + _dot(al, bh) + _dot(ah, bl)


def _dot3_nt(a, b):
    ah, al = _split(a)
    bh, bl = _split(b)
    return _dot_nt(ah, bh) + _dot_nt(al, bh) + _dot_nt(ah, bl)


def _dot2_exact_rhs(a, b_bf16):
    ah, al = _split(a)
    return _dot(ah, b_bf16) + _dot(al, b_bf16)


def _norm_mod(x, gain, shift, scale):
    ms = jnp.mean(x * x, axis=-1, keepdims=True)
    y = x * lax.rsqrt(ms + NORM_EPS) * gain
    return y * (1.0 + scale) + shift


def _adaln_kernel(c_ref, w_ref, b_ref, o_ref):
    c = c_ref[...]
    o_ref[0] = _dot3(jax.nn.silu(c), w_ref[0]) + b_ref[0]


def _adaln(cond, ada_w, ada_b):
    depth, d, n = ada_w.shape
    tn = 1536
    return pl.pallas_call(
        _adaln_kernel,
        grid=(depth, n // tn),
        in_specs=[
            pl.BlockSpec((8, d), lambda l, j: (0, 0)),
            pl.BlockSpec((1, d, tn), lambda l, j: (l, 0, j)),
            pl.BlockSpec((1, 1, tn), lambda l, j: (l, 0, j)),
        ],
        out_specs=pl.BlockSpec((1, 8, tn), lambda l, j: (l, 0, j)),
        out_shape=jax.ShapeDtypeStruct((depth, 8, n), F32),
        compiler_params=_cparams(("arbitrary", "arbitrary")),
        name="adaln",
    )(cond, ada_w, ada_b.reshape(depth, 1, n))


def _head_norm_rope(y, bd, hg, cos, sin):
    ss = _dot2_exact_rhs(y * y, bd)
    y = y * lax.rsqrt(ss * (1.0 / HEAD_DIM) + NORM_EPS) * hg
    lane = lax.broadcasted_iota(jnp.int32, y.shape, 1)
    first_half = (lane % 32) < 16
    partner = jnp.where(first_half, pltpu.roll(y, LANES - 16, axis=1), pltpu.roll(y, 16, axis=1))
    return y * cos + partner * sin


def _inproj_kernel(x_ref, sh_ref, sc_ref, gain_ref, w_ref, cos_ref, sin_ref, hg_ref, bd_ref, *out_refs, plan):
    h = _norm_mod(x_ref[0], gain_ref[...], sh_ref[0], sc_ref[0]).astype(BF16)
    bd = bd_ref[...]
    cos = cos_ref[...]
    sin = sin_ref[...]
    n_blocks = len(plan)
    project = lambda j: _dot(h, w_ref[:, j * LANES:(j + min(2, n_blocks - j)) * LANES])
    y_next = project(0)
    for j0 in range(0, n_blocks, 2):
        width = min(2, n_blocks - j0) * LANES
        y2 = y_next
        if j0 + 2 < n_blocks:
            y_next = project(j0 + 2)
        for jj in range(width // LANES):
            kind, hg_idx, out_idx, out_blk = plan[j0 + jj]
            y = y2[:, jj * LANES:(jj + 1) * LANES]
            o_ref = out_refs[out_idx]
            if kind == "qk":
                y = _head_norm_rope(y, bd, hg_ref[hg_idx:hg_idx + 1, :], cos, sin)
                o_ref[0, :, out_blk * LANES:(out_blk + 1) * LANES] = y.astype(o_ref.dtype)
            elif kind == "vT":
                o_ref[0, out_blk] = y.T.astype(o_ref.dtype)
            else:
                o_ref[0, :, out_blk * LANES:(out_blk + 1) * LANES] = y.astype(o_ref.dtype)


def _inproj(x, shift, scale, gain, w, cos, sin, hg, plan, out_defs, tm):
    b, t, d = x.shape
    n = w.shape[1]
    bd = jnp.asarray(np.kron(np.eye(2), np.ones((HEAD_DIM, HEAD_DIM))), BF16)
    out_shapes, out_specs = [], []
    for kind, nblk, dt in out_defs:
        if kind == "rows":
            out_shapes.append(jax.ShapeDtypeStruct((b, t, nblk * LANES), dt))
            out_specs.append(pl.BlockSpec((1, tm, nblk * LANES), lambda bi, i: (bi, i, 0)))
        else:
            out_shapes.append(jax.ShapeDtypeStruct((b, nblk, LANES, t), dt))
            out_specs.append(pl.BlockSpec((1, nblk, LANES, tm), lambda bi, i: (bi, 0, 0, i)))
    return pl.pallas_call(
        functools.partial(_inproj_kernel, plan=tuple(plan)),
        grid=(b, t // tm),
        in_specs=[
            pl.BlockSpec((1, tm, d), lambda bi, i: (bi, i, 0)),
            pl.BlockSpec((1, 1, d), lambda bi, i: (bi, 0, 0)),
            pl.BlockSpec((1, 1, d), lambda bi, i: (bi, 0, 0)),
            pl.BlockSpec((1, d), lambda bi, i: (0, 0)),
            pl.BlockSpec((d, n), lambda bi, i: (0, 0)),
            pl.BlockSpec((tm, LANES), lambda bi, i: (i, 0)),
            pl.BlockSpec((tm, LANES), lambda bi, i: (i, 0)),
            pl.BlockSpec(hg.shape, lambda bi, i: (0, 0)),
            pl.BlockSpec((LANES, LANES), lambda bi, i: (0, 0)),
        ],
        out_specs=out_specs,
        out_shape=out_shapes,
        compiler_params=_cparams(("parallel", "arbitrary")),
        name="inproj",
    )(x, shift, scale, gain.reshape(1, d), w, cos, sin, hg, bd)


def _rope_tables(n_tokens):
    axis_dim = HEAD_DIM // 2
    freqs = ROPE_THETA ** (-jnp.arange(0, axis_dim, 2, dtype=F32) / axis_dim)
    t = jnp.arange(n_tokens, dtype=jnp.int32)
    lane = np.arange(LANES)
    d = lane % HEAD_DIM
    use_col = (d // 32) == 1
    f_idx = (d % 32) % 16
    sign = np.where((d % 32) < 16, -1.0, 1.0).astype(np.float32)
    pos = jnp.where(use_col[None, :], (t % GRID_W)[:, None], (t // GRID_W)[:, None]).astype(F32)
    ang = pos * freqs[f_idx][None, :]
    return jnp.cos(ang), jnp.sin(ang) * sign[None, :]


def _diffattn_kernel(lam_ref, q_ref, k_ref, vT_ref, subln_ref, o_ref, *s_refs, tq, tk, n_tiles, post_scale):
    n_q = q_ref.shape[1] // tq
    lane = lax.broadcasted_iota(jnp.int32, (tq, LANES), 1)
    sub = DIFF_SUB_BLOCK
    n_sub = tk // sub
    n_buf = len(s_refs)
    ahead = n_buf - 1

    def query_weights(qt):
        q = q_ref[0, qt * tq:(qt + 1) * tq, :]
        zero = jnp.zeros_like(q)
        qq = jnp.concatenate([jnp.where(lane < HEAD_DIM, q, zero), jnp.where(lane >= HEAD_DIM, q, zero)], axis=0)
        return qq.astype(F32).T.astype(BF16)

    qq_t = [query_weights(qt) for qt in range(n_q)]

    def scores(qt, i, s_ref):
        s = _dot(k_ref[0, i * tk:(i + 1) * tk, :], qq_t[qt])
        s_ref[...] = s
        return jnp.max(s, axis=0, keepdims=True)

    def finish(qt, l, acc):
        o = acc / l
        oT = o[:, :tq] - lam_ref[0, 0] * o[:, tq:]
        ms = jnp.mean(oT * oT, axis=0, keepdims=True)
        oT = oT * lax.rsqrt(ms + NORM_EPS) * subln_ref[...] * post_scale
        o_ref[0, qt * tq:(qt + 1) * tq, :] = oT.T.astype(o_ref.dtype)

    items = [(qt, i) for qt in range(n_q) for i in range(n_tiles)]
    cmaxes = {}
    for n in range(min(ahead, len(items))):
        cmaxes[n] = scores(*items[n], s_refs[n % n_buf])
    state = None
    for n, (qt, i) in enumerate(items):
        if i == 0:
            state = (jnp.full((1, 2 * tq), -jnp.inf, F32), jnp.zeros((1, 2 * tq), F32),
                     jnp.zeros((LANES, 2 * tq), F32))
        m, l, acc = state
        m_new = jnp.maximum(m, cmaxes.pop(n))
        alpha = jnp.exp2(m - m_new)
        s_ref = s_refs[n % n_buf]
        if n + ahead < len(items):
            cmaxes[n + ahead] = scores(*items[n + ahead], s_refs[(n + ahead) % n_buf])
        pv, psum = None, None
        for j in range(n_sub):
            p = jnp.exp2(s_ref[j * sub:(j + 1) * sub, :] - m_new)
            ps = jnp.sum(p, axis=0, keepdims=True)
            d = _dot(vT_ref[0, 0, :, i * tk + j * sub:i * tk + (j + 1) * sub], p.astype(BF16))
            psum = ps if psum is None else psum + ps
            pv = d if pv is None else pv + d
        state = (m_new, alpha * l + psum, alpha * acc + pv)
        if i == n_tiles - 1:
            finish(qt, state[1], state[2])


DIFF_KEY_TILE = 1280
DIFF_SUB_BLOCK = 256
DIFF_LOOKAHEAD = 2
DIFF_Q_TILES_PER_STEP = 4


def _diffattn(lam, q_arr, q_blk0, k_arr, k_blk0, vT, subln, post_scale, tq=128):
    b, t_q, _ = q_arr.shape
    t_k = k_arr.shape[1]
    tk = DIFF_KEY_TILE if (t_k % DIFF_KEY_TILE == 0 and t_k > DIFF_KEY_TILE) else 256
    rows = tq * min(DIFF_Q_TILES_PER_STEP, t_q // tq)
    return pl.pallas_call(
        functools.partial(_diffattn_kernel, tq=tq, tk=tk, n_tiles=t_k // tk, post_scale=post_scale),
        grid=(b, A_HEADS, t_q // rows),
        in_specs=[
            pl.BlockSpec(memory_space=pltpu.SMEM),
            pl.BlockSpec((1, rows, LANES), lambda bi, h, i: (bi, i, q_blk0 + h)),
            pl.BlockSpec((1, t_k, LANES), lambda bi, h, i: (bi, 0, k_blk0 + h)),
            pl.BlockSpec((1, 1, LANES, t_k), lambda bi, h, i: (bi, h, 0, 0)),
            pl.BlockSpec((LANES, 1), lambda bi, h, i: (0, 0)),
        ],
        out_specs=pl.BlockSpec((1, rows, LANES), lambda bi, h, i: (bi, i, h)),
        out_shape=jax.ShapeDtypeStruct((b, t_q, A_HEADS * LANES), BF16),
        scratch_shapes=[pltpu.VMEM((tk, 2 * tq), F32)] * (DIFF_LOOKAHEAD + 1),
        compiler_params=_cparams(("parallel", "arbitrary", "arbitrary")),
        name="diffattn",
    )(lam, q_arr, k_arr, vT, subln.reshape(LANES, 1))


WINDOW_BLOCKS_PER_STEP = 4

def _window_kernel(sink_ref, q_ref, kc_ref, vc_ref, *rest, with_win, n_blocks):
    if with_win:
        kp_ref, k0_ref, kn_ref, vp_ref, v0_ref, vn_ref, o_ref = rest
    else:
        (o_ref,) = rest
    i = pl.program_id(1)
    tq = WINDOW
    n_sb = q_ref.shape[1] // tq
    n_ctx = kc_ref.shape[1]
    lane = lax.broadcasted_iota(jnp.int32, (1, LANES), 1)
    lo = lane < HEAD_DIM
    if with_win:
        r = lax.broadcasted_iota(jnp.int32, (tq, tq), 0)
        c = lax.broadcasted_iota(jnp.int32, (tq, tq), 1)
    problems = [(sb, g) for sb in range(n_sb) for g in range(2)]
    scores, vcats, valids = [], [], []
    for sb, g in problems:
        gs = slice(g * LANES, (g + 1) * LANES)
        qs = slice(sb * tq, (sb + 1) * tq)
        if with_win:
            k_blocks = [kp_ref[0, :, gs]] + [k0_ref[0, t * tq:(t + 1) * tq, gs] for t in range(n_sb)] + [kn_ref[0, :, gs]]
            v_blocks = [vp_ref[0, :, gs]] + [v0_ref[0, t * tq:(t + 1) * tq, gs] for t in range(n_sb)] + [vn_ref[0, :, gs]]
            kcat = jnp.concatenate(k_blocks[sb:sb + 3] + [kc_ref[0, :, gs]], axis=0)
            vcat = jnp.concatenate(v_blocks[sb:sb + 3] + [vc_ref[0, :, gs]], axis=0)
            blk = i * n_sb + sb
            ok_prev = jnp.logical_and(c >= r, blk > 0)
            ok_next = jnp.logical_and(c <= r, blk < n_blocks - 1)
            valids.append(jnp.concatenate([ok_prev, jnp.ones((tq, tq), jnp.bool_), ok_next,
                                           jnp.ones((tq, n_ctx), jnp.bool_)], axis=1))
        else:
            kcat = kc_ref[0, :, gs]
            vcat = vc_ref[0, :, gs]
        q_rows = []
        for cb in range(2):
            qb = q_ref[0, qs, (g * 2 + cb) * LANES:(g * 2 + cb + 1) * LANES]
            zq = jnp.zeros_like(qb)
            q_rows += [jnp.where(lo, qb, zq), jnp.where(lo, zq, qb)]
        scores.append(_dot_nt(jnp.concatenate(q_rows, axis=0), kcat))
        vcats.append(vcat)
    for n, (sb, g) in enumerate(problems):
        s = scores[n]
        if with_win:
            s = jnp.where(jnp.concatenate([valids[n]] * 4, axis=0), s, NEG_INF)
        snk = jnp.concatenate([jnp.broadcast_to(sink_ref[g * 4 + h:g * 4 + h + 1, 0:1], (tq, 1)) for h in range(4)],
                              axis=0)
        m = jnp.maximum(jnp.max(s, axis=-1, keepdims=True), snk)
        e = jnp.exp(s - m)
        denom = jnp.sum(e, axis=-1, keepdims=True) + jnp.exp(snk - m)
        o = _dot(e.astype(BF16), vcats[n]) / denom
        for cb in range(2):
            out = jnp.where(lo, o[(2 * cb) * tq:(2 * cb + 1) * tq], o[(2 * cb + 1) * tq:(2 * cb + 2) * tq])
            o_ref[0, sb * tq:(sb + 1) * tq, (g * 2 + cb) * LANES:(g * 2 + cb + 1) * LANES] = out.astype(o_ref.dtype)


def _window_attn(sink, q_arr, q_blk0, k_blk0, v_blk0, ctx_arr, with_win):
    b, t_q, _ = q_arr.shape
    n_ctx = ctx_arr.shape[1]
    n_sb = min(WINDOW_BLOCKS_PER_STEP, t_q // WINDOW)
    rows = n_sb * WINDOW
    nb = t_q // WINDOW
    sink_tab = jnp.broadcast_to(sink.astype(F32).reshape(8, 1), (8, LANES))
    in_specs = [
        pl.BlockSpec((8, LANES), lambda bi, i: (0, 0)),
        pl.BlockSpec((1, rows, 4 * LANES), lambda bi, i: (bi, i, q_blk0 // 4)),
        pl.BlockSpec((1, n_ctx, 2 * LANES), lambda bi, i: (bi, 0, k_blk0 // 2)),
        pl.BlockSpec((1, n_ctx, 2 * LANES), lambda bi, i: (bi, 0, v_blk0 // 2)),
    ]
    args = [sink_tab, q_arr, ctx_arr, ctx_arr]
    if with_win:
        for blk0 in (k_blk0, v_blk0):
            in_specs += [
                pl.BlockSpec((1, WINDOW, 2 * LANES), lambda bi, i, c=blk0 // 2: (bi, jnp.maximum(i * n_sb - 1, 0), c)),
                pl.BlockSpec((1, rows, 2 * LANES), lambda bi, i, c=blk0 // 2: (bi, i, c)),
                pl.BlockSpec((1, WINDOW, 2 * LANES),
                             lambda bi, i, c=blk0 // 2: (bi, jnp.minimum((i + 1) * n_sb, nb - 1), c)),
            ]
            args += [q_arr, q_arr, q_arr]
    return pl.pallas_call(
        functools.partial(_window_kernel, with_win=with_win, n_blocks=nb),
        grid=(b, t_q // rows),
        in_specs=in_specs,
        out_specs=pl.BlockSpec((1, rows, 4 * LANES), lambda bi, i: (bi, i, 0)),
        out_shape=jax.ShapeDtypeStruct((b, t_q, 4 * LANES), BF16),
        compiler_params=_cparams(("parallel", "arbitrary")),
        name="window_attn",
    )(*args)


FFN_ROW_TILE = 512


def _out_ffn_kernel(x_ref, a_ref, b_ref, woa_ref, wob_ref, gate_ref, ng_ref, fsh_ref, fsc_ref, fg_ref,
                    wg_ref, wu_ref, wd_ref, o_ref):
    o = _dot(a_ref[0], woa_ref[...]) + _dot(b_ref[0], wob_ref[...])
    x1 = x_ref[0] + gate_ref[0] * o
    h = _norm_mod(x1, ng_ref[...], fsh_ref[0], fsc_ref[0]).astype(BF16)
    act = (jax.nn.silu(_dot(h, wg_ref[...])) * _dot(h, wu_ref[...])).astype(BF16)
    o_ref[0] = x1 + fg_ref[0] * _dot(act, wd_ref[...])


def _out_ffn(x, mix_a, mix_b, wo, gate, ng, fsh, fsc, fg, wg, wu, wd, tm):
    b, t, d = x.shape
    na, nb_ = mix_a.shape[-1], mix_b.shape[-1]
    dff = wg.shape[1]
    row = lambda bi, i: (bi, i, 0)
    vec = lambda bi, i: (bi, 0, 0)
    const = lambda bi, i: (0, 0)
    resident = functools.partial(pl.BlockSpec, index_map=const, pipeline_mode=pl.Buffered(1))
    return pl.pallas_call(
        _out_ffn_kernel,
        grid=(b, t // tm),
        in_specs=[
            pl.BlockSpec((1, tm, d), row),
            pl.BlockSpec((1, tm, na), row),
            pl.BlockSpec((1, tm, nb_), row),
            resident((na, d)),
            resident((nb_, d)),
            pl.BlockSpec((1, 1, d), vec),
            resident((1, d)),
            pl.BlockSpec((1, 1, d), vec),
            pl.BlockSpec((1, 1, d), vec),
            pl.BlockSpec((1, 1, d), vec),
            resident((d, dff)),
            resident((d, dff)),
            resident((dff, d)),
        ],
        out_specs=pl.BlockSpec((1, tm, d), row),
        out_shape=jax.ShapeDtypeStruct((b, t, d), F32),
        compiler_params=_cparams(("parallel", "arbitrary")),
        name="out_ffn",
    )(x, mix_a, mix_b, wo[:na], wo[na:], gate, ng.reshape(1, d), fsh, fsc, fg, wg, wu, wd)


NA_ROWS_PER_STEP = 16


def _natten_kernel(q_ref, k_ref, v_ref, kc_ref, vc_ref, bias_ref, o_ref, *, n_rows):
    step = pl.program_id(2)
    n_win = NA_ROWS * GRID_W
    lane = lax.broadcasted_iota(jnp.int32, (1, LANES), 1)
    lo = lane < HEAD_DIM
    keeps = (lo, jnp.logical_not(lo))
    q_all = q_ref[0]
    zq = jnp.zeros_like(q_all)
    q_half = [jnp.where(keep, q_all, zq) for keep in keeps]
    s_ctx = [_dot_nt(qh, kc_ref[0]) for qh in q_half]
    row_info, s_win = [], []
    for rr in range(NA_ROWS_PER_STEP):
        i = step * NA_ROWS_PER_STEP + rr
        r0 = jnp.clip(i - NA_ROWS // 2, 0, n_rows - NA_ROWS)
        koff = pl.multiple_of(r0 * GRID_W, GRID_W)
        row_info.append((i - r0, koff))
        k_win = k_ref[0, pl.ds(koff, n_win), :]
        for half in range(2):
            s_win.append(_dot_nt(q_half[half][rr * GRID_W:(rr + 1) * GRID_W], k_win))
    e_win, e_ctx, denom = [], [[], []], [[], []]
    for rr in range(NA_ROWS_PER_STEP):
        qs = slice(rr * GRID_W, (rr + 1) * GRID_W)
        for half in range(2):
            sw = s_win[2 * rr + half] + bias_ref[row_info[rr][0], half]
            sc = s_ctx[half][qs]
            m = jnp.maximum(jnp.max(sw, axis=-1, keepdims=True), jnp.max(sc, axis=-1, keepdims=True))
            ew, ec = jnp.exp(sw - m), jnp.exp(sc - m)
            e_win.append(ew.astype(BF16))
            e_ctx[half].append(ec.astype(BF16))
            denom[half].append(jnp.sum(ew, axis=-1, keepdims=True) + jnp.sum(ec, axis=-1, keepdims=True))
    vc = vc_ref[0]
    zc = jnp.zeros_like(vc)
    o_ctx = [_dot(jnp.concatenate(e_ctx[half], axis=0), jnp.where(keeps[half], vc, zc)) for half in range(2)]
    for rr in range(NA_ROWS_PER_STEP):
        qs = slice(rr * GRID_W, (rr + 1) * GRID_W)
        v_win = v_ref[0, pl.ds(row_info[rr][1], n_win), :]
        zv = jnp.zeros_like(v_win)
        out = None
        for half in range(2):
            o = (_dot(e_win[2 * rr + half], jnp.where(keeps[half], v_win, zv)) + o_ctx[half][qs]) / denom[half][rr]
            out = o if out is None else out + o
        o_ref[0, qs, :] = out.astype(o_ref.dtype)


def _natten_bias(rpb):
    n_heads = rpb.shape[0]
    qj = np.arange(GRID_W)[:, None]
    kj = np.arange(GRID_W)[None, :]
    cstart = np.clip(qj - NA_COLS // 2, 0, GRID_W - NA_COLS)
    in_win = (kj >= cstart) & (kj < cstart + NA_COLS)
    pad = GRID_W - NA_COLS
    rpb_p = jnp.pad(rpb.astype(F32), ((0, 0), (0, 0), (pad, pad)))
    toep = jnp.stack([rpb_p[:, :, pad + NA_COLS - 1 - q:pad + NA_COLS - 1 - q + GRID_W] for q in range(GRID_W)],
                     axis=2)
    toep = jnp.where(in_win[None, None], toep, NEG_INF)
    per_class = [jnp.moveaxis(toep[:, NA_ROWS - 1 - d:2 * NA_ROWS - 1 - d], 1, 2) for d in range(NA_ROWS)]
    return jnp.stack(per_class).reshape(NA_ROWS, n_heads, GRID_W, NA_ROWS * GRID_W)


def _natten(qkv, qkv_ctx, bias):
    b, s_len, _ = qkv.shape
    n_ctx = qkv_ctx.shape[1]
    n_rows = s_len // GRID_W
    tq = NA_ROWS_PER_STEP * GRID_W
    return pl.pallas_call(
        functools.partial(_natten_kernel, n_rows=n_rows),
        grid=(b, 4, n_rows // NA_ROWS_PER_STEP),
        in_specs=[
            pl.BlockSpec((1, tq, LANES), lambda bi, hp, i: (bi, i, hp)),
            pl.BlockSpec((1, s_len, LANES), lambda bi, hp, i: (bi, 0, 4 + hp)),
            pl.BlockSpec((1, s_len, LANES), lambda bi, hp, i: (bi, 0, 8 + hp)),
            pl.BlockSpec((1, n_ctx, LANES), lambda bi, hp, i: (bi, 0, 4 + hp)),
            pl.BlockSpec((1, n_ctx, LANES), lambda bi, hp, i: (bi, 0, 8 + hp)),
            pl.BlockSpec((NA_ROWS, 2, GRID_W, NA_ROWS * GRID_W), lambda bi, hp, i: (0, hp, 0, 0)),
        ],
        out_specs=pl.BlockSpec((1, tq, LANES), lambda bi, hp, i: (bi, i, hp)),
        out_shape=jax.ShapeDtypeStruct((b, s_len, 4 * LANES), BF16),
        compiler_params=_cparams(("parallel", "arbitrary", "arbitrary")),
        name="natten",
    )(qkv, qkv, qkv, qkv_ctx, qkv_ctx, bias)


def _head_sum(x, bd):
    return _dot2_exact_rhs(x, bd)


def _rwkv_prep_kernel(x_ref, xp_ref, xn_ref, mu_ref, w0_ref, w2_ref, a0_ref, a2_ref, g2_ref, kk_ref, bd_ref,
                      rkvk_ref, ld_ref, a_ref, g_ref):
    i = pl.program_id(1)
    n_t = pl.num_programs(1)
    x = x_ref[0]
    tm = x.shape[0]
    prev_row = jnp.where(i > 0, xp_ref[0, 7:8, :], 0.0)
    next_row = jnp.where(i < n_t - 1, xn_ref[0, 0:1, :], 0.0)
    rows = lax.broadcasted_iota(jnp.int32, (tm, 1), 0)
    x_prev = jnp.where(rows == 0, prev_row, pltpu.roll(x, 1, axis=0))
    x_next = jnp.where(rows == tm - 1, next_row, pltpu.roll(x, tm - 1, axis=0))
    xs = x + mu_ref[...] * (0.5 * (x_prev + x_next) - x)
    w = D_WIDTH
    k = xs[:, w:2 * w]
    rkvk_ref[0, :, 0:3 * w] = xs[:, 0:3 * w]
    kk = k * kk_ref[...]
    nrm = jnp.sqrt(jnp.concatenate([_head_sum(kk[:, j * LANES:(j + 1) * LANES] ** 2, bd_ref[...])
                                    for j in range(w // LANES)], axis=1))
    rkvk_ref[0, :, 3 * w:4 * w] = kk / jnp.maximum(nrm, 1e-12)
    wd = xs[:, 3 * w:3 * w + LANES]
    ad = xs[:, 3 * w + LANES:3 * w + 2 * LANES]
    gd = xs[:, 3 * w + 2 * LANES:3 * w + 3 * LANES]
    z = w0_ref[...] + _dot3(jnp.tanh(wd), w2_ref[...])
    ld_ref[0] = (-math.exp(-0.5)) * jax.nn.sigmoid(z)
    a_ref[0] = jax.nn.sigmoid(a0_ref[...] + _dot3(ad, a2_ref[...]))
    g_ref[0] = _dot3(jax.nn.sigmoid(gd), g2_ref[...])


def _rwkv_prep(dcols, mu, w0, w2, a0, a2, g2, k_k, tm):
    b, t, n = dcols.shape
    w = D_WIDTH
    bd = jnp.asarray(np.kron(np.eye(2), np.ones((HEAD_DIM, HEAD_DIM))), BF16)
    pad_rows = lambda m, r0: jnp.zeros((LANES, m.shape[1]), F32).at[r0:r0 + m.shape[0]].set(m)
    w2p = jnp.concatenate([pad_rows(w2[0], 0), pad_rows(w2[1], 32)], axis=1)
    a2p = jnp.concatenate([pad_rows(a2[0], 0), pad_rows(a2[1], 32)], axis=1)
    g2p = pad_rows(g2, 0)
    const = lambda bi, i: (0, 0)
    t8 = tm // 8
    return pl.pallas_call(
        _rwkv_prep_kernel,
        grid=(b, t // tm),
        in_specs=[
            pl.BlockSpec((1, tm, n), lambda bi, i: (bi, i, 0)),
            pl.BlockSpec((1, 8, n), lambda bi, i: (bi, jnp.maximum(i * t8 - 1, 0), 0)),
            pl.BlockSpec((1, 8, n), lambda bi, i: (bi, jnp.minimum((i + 1) * t8, t // 8 - 1), 0)),
            pl.BlockSpec((1, n), const),
            pl.BlockSpec((1, 2 * w), const),
            pl.BlockSpec((LANES, 2 * w), const),
            pl.BlockSpec((1, 2 * w), const),
            pl.BlockSpec((LANES, 2 * w), const),
            pl.BlockSpec((LANES, w), const),
            pl.BlockSpec((1, w), const),
            pl.BlockSpec((LANES, LANES), const),
        ],
        out_specs=[
            pl.BlockSpec((1, tm, 4 * w), lambda bi, i: (bi, i, 0)),
            pl.BlockSpec((1, tm, 2 * w), lambda bi, i: (bi, i, 0)),
            pl.BlockSpec((1, tm, 2 * w), lambda bi, i: (bi, i, 0)),
            pl.BlockSpec((1, tm, w), lambda bi, i: (bi, i, 0)),
        ],
        out_shape=[
            jax.ShapeDtypeStruct((b, t, 4 * w), F32),
            jax.ShapeDtypeStruct((b, t, 2 * w), F32),
            jax.ShapeDtypeStruct((b, t, 2 * w), F32),
            jax.ShapeDtypeStruct((b, t, w), F32),
        ],
        compiler_params=_cparams(("parallel", "arbitrary")),
        name="rwkv_prep",
    )(dcols, dcols, dcols, mu.reshape(1, n), w0.reshape(1, 2 * w), w2p, a0.reshape(1, 2 * w), a2p, g2p,
      k_k.reshape(1, w), bd)


SCAN_CHUNK = 64
SCAN_TILE = 512


def _split3(x):
    hi = x.astype(BF16)
    r1 = x - hi.astype(F32)
    mid = r1.astype(BF16)
    lo = (r1 - mid.astype(F32)).astype(BF16)
    return hi, mid, lo


def _mm(a, b, precise):
    return _dot3(a, b) if precise else _dot(a.astype(BF16), b.astype(BF16))


def _mm_nt(a, b, precise):
    return _dot3_nt(a, b) if precise else _dot_nt(a.astype(BF16), b.astype(BF16))


def _blockwise_mm(x, y, bd_mask, precise):
    n_blk = x.shape[1] // SCAN_CHUNK
    expand = lambda m: jnp.where(bd_mask, jnp.concatenate([m] * n_blk, axis=0), jnp.zeros((), m.dtype))
    if not precise:
        return _dot(x.astype(BF16), expand(y.astype(BF16)))
    xh, xl = _split(x)
    yh, yl = _split(y)
    ybd = expand(yh)
    return _dot(xh, ybd) + _dot(xl, ybd) + _dot(xh, expand(yl))


SCAN_PRECISE_INVERSE = False
SCAN_PRECISE_STATE = False
SCAN_PRECISE_OTHER = False


def _rwkv_scan_kernel(*refs, n_chunks):
    fwd_refs, rev_refs = refs[0:6], refs[6:12]
    ka_ref, s0_ref = refs[12:14]
    y_refs = refs[14:16]
    sf_ref, state_ref, t64_ref, t128_ref = refs[16:20]
    it = pl.program_id(2)
    last = pl.num_programs(2) - 1
    c = SCAN_CHUNK
    n_units = n_chunks // 2

    @pl.when(it == 0)
    def _():
        state_ref[...] = s0_ref[0, :, 0]
        t64_ref[...] = jnp.zeros_like(t64_ref)
        t128_ref[...] = jnp.zeros_like(t128_ref)

    ri = lax.broadcasted_iota(jnp.int32, (c, c), 0)
    ci = lax.broadcasted_iota(jnp.int32, (c, c), 1)
    eye = (ri == ci).astype(F32)
    tile_l = lambda m, n: jnp.concatenate([m] * n, axis=1)
    eye4 = tile_l(eye, 4)
    blk = lambda n: ((ri // n) == (ci // n)).astype(F32)
    m8, m16, m32 = tile_l(blk(8), 4), tile_l(blk(16), 4), tile_l(blk(32), 4)
    r4 = lax.broadcasted_iota(jnp.int32, (4 * c, 4 * c), 0)
    c4 = lax.broadcasted_iota(jnp.int32, (4 * c, 4 * c), 1)
    bd4 = (r4 // c) == (c4 // c)
    lane = lax.broadcasted_iota(jnp.int32, (1, LANES), 1)
    hm_lo = (lane < HEAD_DIM).astype(F32)
    hm_hi = 1.0 - hm_lo
    r2 = lax.broadcasted_iota(jnp.int32, (LANES, LANES), 0)
    c2 = lax.broadcasted_iota(jnp.int32, (LANES, LANES), 1)
    bd128 = ((r2 // HEAD_DIM) == (c2 // HEAD_DIM)).astype(F32)
    eye128 = (r2 == c2).astype(F32)
    ka = ka_ref[...]
    vstack = lambda *xs: jnp.concatenate(xs, axis=0)
    hstack = lambda *xs: jnp.concatenate(xs, axis=1)
    split_heads = lambda m: vstack(m * hm_lo, m * hm_hi)
    po = SCAN_PRECISE_OTHER

    rows = [slice(j * c, (j + 1) * c) for j in range(n_chunks)]
    masks = []
    for z in range(2):
        strict = ((ci > ri) if z == 1 else (ci < ri)).astype(F32)
        incl = strict + eye
        masks.append(dict(strict2=tile_l(strict, 2), incl2=tile_l(incl, 2), tri_b=incl.astype(BF16)))

    states = [state_ref[0], state_ref[1]]
    pending = {}

    def chain_step(step):
        for z in range(2):
            j = step if z == 0 else n_chunks - 1 - step
            idx = z * n_chunks + j
            s_in = states[z]
            s_new = _mm(s_in, t128_ref[idx, 0], SCAN_PRECISE_STATE) + t128_ref[idx, 1]
            states[z] = jnp.where(it > 0, s_new, s_in)
            wr = _mm_nt(vstack(t64_ref[idx, 0], t64_ref[idx, 1]), s_in, SCAN_PRECISE_STATE)
            pending[(step, z)] = (idx, j, wr)

    def chain_outputs(step):
        for z in range(2):
            idx, j, wr = pending.pop((step, z))
            u_full = wr[:c] + t64_ref[idx, 2]
            y_refs[z][0, rows[j], :] = wr[c:] + t64_ref[idx, 3] + _mm(t64_ref[idx, 4], split_heads(u_full), po)

    chain_iter = iter(range(n_chunks + 1))

    def chain_slot():
        k = next(chain_iter, None)
        if k is None:
            return
        if k > 0:
            chain_outputs(k - 1)
        if k < n_chunks:
            chain_step(k)

    cums = []
    for z, in_refs in enumerate((fwd_refs, rev_refs)):
        h3 = _split3(hstack(*[in_refs[4][0, rw, :] for rw in rows]))
        tri_b = masks[z]["tri_b"]
        cums.append(_dot(tri_b, h3[0]) + _dot(tri_b, h3[1]) + _dot(tri_b, h3[2]))
    chain_slot()

    chunks = []
    for z, in_refs in enumerate((fwd_refs, rev_refs)):
        r_ref, k_ref, v_ref, kk_ref, ld_ref, a_ref = in_refs
        for j, rw in enumerate(rows):
            r, k, v, kk = r_ref[0, rw, :], k_ref[0, rw, :], v_ref[0, rw, :], kk_ref[0, rw, :]
            ld, a = ld_ref[0, rw, :], a_ref[0, rw, :]
            cum = cums[z][:, j * LANES:(j + 1) * LANES]
            pinv = jnp.exp(-cum)
            p_end = jnp.exp(cum[0:1, :] if z == 1 else cum[c - 1:c, :])
            r_t = r * jnp.exp(cum)
            a_t = -kk * jnp.exp(cum - ld)
            b_t = kk * a * pinv
            k_t = k * (1.0 + (a - 1.0) * ka) * pinv
            chunks.append(dict(z=z, j=j, r_t=r_t, a_t=a_t, b_t=b_t, k_t=k_t, v=v, p_end=p_end))
    for ch in chunks:
        g = _mm_nt(vstack(ch["a_t"], ch["r_t"]), vstack(split_heads(ch["b_t"]), split_heads(ch["k_t"])), po)
        mk = masks[ch["z"]]
        ch.update(l_ab=g[:c, :2 * c] * mk["strict2"], l_ak=g[:c, 2 * c:] * mk["strict2"],
                  m_rb=g[c:, :2 * c] * mk["incl2"], m_rk=g[c:, 2 * c:] * mk["incl2"])
    chain_slot()
    for ch in chunks:
        lv_y0 = _mm(vstack(ch["l_ak"], ch["m_rk"]), split_heads(ch["v"]), po)
        ch.update(lv=lv_y0[:c], y0=lv_y0[c:])
    chain_slot()

    mm4 = lambda x, y: _blockwise_mm(x, y, bd4, SCAN_PRECISE_INVERSE)
    l4s = [hstack(chunks[2 * u]["l_ab"], chunks[2 * u + 1]["l_ab"]) for u in range(2 * n_units)]
    p1s = [l4 * m8 for l4 in l4s]
    p2s = [mm4(p1, p1) for p1 in p1s]
    chain_slot()
    xs = [eye4 + p1 for p1 in p1s]
    xs = [x + mm4(x, p2) for x, p2 in zip(xs, p2s)]
    chain_slot()
    p4s = [mm4(p2, p2) for p2 in p2s]
    chain_slot()
    xs = [x + mm4(x, p4) for x, p4 in zip(xs, p4s)]
    chain_slot()
    for inner, outer in ((m8, m16), (m16, m32), (m32, None)):
        sel = (1.0 - inner) if outer is None else (outer - inner)
        ys = [mm4(x, l4 * sel) for x, l4 in zip(xs, l4s)]
        chain_slot()
        xs = [x + mm4(y, x) for x, y in zip(xs, ys)]
        chain_slot()

    for i, ch in enumerate(chunks):
        t_inv = xs[i // 2][:, (i % 2) * LANES:(i % 2 + 1) * LANES]
        w_u0 = _mm(t_inv, hstack(split_heads(ch["a_t"]), split_heads(ch["lv"])), po)
        ch.update(w=w_u0[:, :LANES], u0=w_u0[:, LANES:])
    chain_slot()
    for ch in chunks:
        m = _mm(ch["w"].T, ch["b_t"], po)
        ch.update(a_mat=(eye128 + m * bd128) * ch["p_end"])
    chain_slot()
    for ch in chunks:
        b_raw = _mm(vstack(ch["u0"], ch["v"]).T, vstack(ch["b_t"], ch["k_t"]), po)
        ch.update(b_mat=b_raw * bd128 * ch["p_end"])
    while next(chain_iter, None) is not None:
        raise AssertionError("not enough stages to interleave the state chain")
    assert not pending

    for idx, ch in enumerate(chunks):
        for q, name in enumerate(("w", "r_t", "u0", "y0", "m_rb")):
            t64_ref[idx, q] = ch[name]
        t128_ref[idx, 0] = ch["a_mat"]
        t128_ref[idx, 1] = ch["b_mat"]
    state_ref[0] = states[0]
    state_ref[1] = states[1]

    @pl.when(it == last)
    def _():
        sf_ref[0, :, 0] = state_ref[...]


def _rwkv_scan(rkvk, ld, a, k_a, s0, tile):
    b, t, _ = rkvk.shape
    n_t = t // tile
    n_chunks = tile // SCAN_CHUNK
    tile_in = lambda i: jnp.minimum(i, n_t - 1)
    tile_out = lambda i: jnp.maximum(i - 1, 0)
    blk = lambda col0, rev, pick: pl.BlockSpec(
        (1, tile, LANES), (lambda bi, hp, i: (bi, n_t - 1 - pick(i), col0 + hp)) if rev else
        (lambda bi, hp, i: (bi, pick(i), col0 + hp)))
    state_spec = pl.BlockSpec((1, 2, 1, LANES, LANES), lambda bi, hp, i: (bi, 0, hp, 0, 0))
    in_specs, args = [], []
    for z in range(2):
        in_specs += [blk(c0, z, tile_in) for c0 in (0, 4, 8, 12, 4 * z, 4 * z)]
        args += [rkvk, rkvk, rkvk, rkvk, ld, a]
    return pl.pallas_call(
        functools.partial(_rwkv_scan_kernel, n_chunks=n_chunks),
        grid=(b, 4, n_t + 1),
        in_specs=in_specs + [pl.BlockSpec((1, LANES), lambda bi, hp, i: (0, hp)), state_spec],
        out_specs=[blk(0, 0, tile_out), blk(0, 1, tile_out), state_spec],
        out_shape=[
            jax.ShapeDtypeStruct((b, t, D_WIDTH), F32),
            jax.ShapeDtypeStruct((b, t, D_WIDTH), F32),
            jax.ShapeDtypeStruct((b, 2, 4, LANES, LANES), F32),
        ],
        scratch_shapes=[pltpu.VMEM((2, LANES, LANES), F32),
                        pltpu.VMEM((2 * n_chunks, 5, SCAN_CHUNK, LANES), F32),
                        pltpu.VMEM((2 * n_chunks, 2, LANES, LANES), F32)],
        compiler_params=_cparams(("parallel", "arbitrary", "arbitrary")),
        name="rwkv_scan",
    )(*args, k_a.reshape(1, D_WIDTH), s0)


def _rwkv_out_kernel(yf_ref, yr_ref, rkvk_ref, a_ref, g_ref, ka_ref, rk_ref, lnw_ref, lnb_ref, bd_ref, o_ref):
    w = D_WIDTH
    bd = bd_ref[...]
    for j in range(w // LANES):
        cs = slice(j * LANES, (j + 1) * LANES)
        y = yf_ref[0, :, cs] + yr_ref[0, :, cs]
        mean = _head_sum(y, bd) * (1.0 / HEAD_DIM)
        yc = y - mean
        var = _head_sum(yc * yc, bd) * (1.0 / HEAD_DIM)
        yn = yc * lax.rsqrt(var + D_GN_EPS) * lnw_ref[:, cs] + lnb_ref[:, cs]
        r = rkvk_ref[0, :, cs]
        k = rkvk_ref[0, :, w + j * LANES:w + (j + 1) * LANES]
        v = rkvk_ref[0, :, 2 * w + j * LANES:2 * w + (j + 1) * LANES]
        bonus = jnp.zeros_like(y)
        for z in range(2):
            a = a_ref[0, :, z * w + j * LANES:z * w + (j + 1) * LANES]
            k_dir = k * (1.0 + (a - 1.0) * ka_ref[:, cs])
            bonus = bonus + _head_sum(r * k_dir * rk_ref[:, cs], bd) * v
        o_ref[0, :, cs] = ((yn + bonus) * g_ref[0, :, cs]).astype(o_ref.dtype)


def _rwkv_out(y_f, y_r, rkvk, a, g, k_a, r_k, ln_w, ln_b, tm):
    b, t, w = y_f.shape
    bd = jnp.asarray(np.kron(np.eye(2), np.ones((HEAD_DIM, HEAD_DIM))), BF16)
    const = lambda bi, i: (0, 0)
    vec = lambda p: p.reshape(1, w).astype(F32)
    return pl.pallas_call(
        _rwkv_out_kernel,
        grid=(b, t // tm),
        in_specs=[
            pl.BlockSpec((1, tm, w), lambda bi, i: (bi, i, 0)),
            pl.BlockSpec((1, tm, w), lambda bi, i: (bi, i, 0)),
            pl.BlockSpec((1, tm, 4 * w), lambda bi, i: (bi, i, 0)),
            pl.BlockSpec((1, tm, 2 * w), lambda bi, i: (bi, i, 0)),
            pl.BlockSpec((1, tm, w), lambda bi, i: (bi, i, 0)),
            pl.BlockSpec((1, w), const),
            pl.BlockSpec((1, w), const),
            pl.BlockSpec((1, w), const),
            pl.BlockSpec((1, w), const),
            pl.BlockSpec((LANES, LANES), const),
        ],
        out_specs=pl.BlockSpec((1, tm, w), lambda bi, i: (bi, i, 0)),
        out_shape=jax.ShapeDtypeStruct((b, t, w), BF16),
        compiler_params=_cparams(("parallel", "arbitrary")),
        name="rwkv_out",
    )(y_f, y_r, rkvk, a, g, vec(k_a), vec(r_k), vec(ln_w), vec(ln_b), bd)


PROJ_ROW_TILE = 512
PREP_ROW_TILE = 256

def _mod_vectors(mods_layer, b):
    d = D_MODEL
    lat = [mods_layer[:b, j * d:(j + 1) * d].reshape(b, 1, d) for j in range(6)]
    ctx = [jnp.broadcast_to(mods_layer[b:b + 1, j * d:(j + 1) * d].reshape(1, 1, d), (b, 1, d)) for j in range(6)]
    return lat, ctx


def _tile2(v):
    return jnp.concatenate([v, v]).astype(F32)


def _ab_layer(x_lat, x_ctx, mods_layer, layer, norm_mix, norm_ffn, wg, wu, wd, w_in, w_out,
              a_qn, a_kn, a_lam, a_subln, b_qn, b_kn, b_sink, rope_lat, rope_ctx, need_ctx):
    b, s_len, d = x_lat.shape
    (sh_l, sc_l, g_l, fsh_l, fsc_l, fg_l), (sh_c, sc_c, g_c, fsh_c, fsc_c, fg_c) = _mod_vectors(mods_layer, b)
    lambda_init = 0.8 - 0.6 * math.exp(-0.3 * layer)

    aw = A_HEADS * LANES
    aq, ak, av = w_in[:, :aw], w_in[:, aw:2 * aw], w_in[:, 2 * aw:3 * aw]
    bq = w_in[:, 3 * aw:4 * aw]
    bk = w_in[:, 4 * aw:4 * aw + LANES]
    bv = w_in[:, 4 * aw + LANES:4 * aw + 2 * LANES]
    dup = lambda w: jnp.concatenate([w[:, :64], w[:, :64], w[:, 64:], w[:, 64:]], axis=1)
    w_cat = jnp.concatenate([aq, ak, bq, dup(bk), dup(bv), av], axis=1).astype(BF16)
    scale = HEAD_DIM ** -0.5
    hg = jnp.stack([_tile2(a_qn) * (scale * LOG2E), _tile2(a_kn), _tile2(b_qn) * scale, _tile2(b_kn)]
                   + [jnp.zeros((LANES,), F32)] * 4)
    plan = ([("qk", 0, 0, j) for j in range(4)] + [("qk", 1, 0, 4 + j) for j in range(4)]
            + [("qk", 2, 0, 8 + j) for j in range(4)] + [("qk", 3, 0, 12 + j) for j in range(2)]
            + [("v", 0, 0, 14 + j) for j in range(2)] + [("vT", 0, 1, j) for j in range(4)])
    out_defs = [("rows", 16, BF16), ("vT", 4, BF16)]
    qkv_l, avT_l = _inproj(x_lat, sh_l, sc_l, norm_mix, w_cat, rope_lat[0], rope_lat[1], hg, plan, out_defs,
                           tm=PROJ_ROW_TILE)
    qkv_c, avT_c = _inproj(x_ctx, sh_c, sc_c, norm_mix, w_cat, rope_ctx[0], rope_ctx[1], hg, plan, out_defs,
                           tm=x_ctx.shape[1])

    lam_f = a_lam.astype(F32)
    lam = (jnp.exp(jnp.sum(lam_f[0] * lam_f[1])) - jnp.exp(jnp.sum(lam_f[2] * lam_f[3])) + lambda_init).reshape(1, 1)
    post = 1.0 - lambda_init
    ak_all = jnp.concatenate([qkv_c[:, :, 4 * LANES:8 * LANES], qkv_l[:, :, 4 * LANES:8 * LANES]], axis=1)
    avT_all = jnp.concatenate([avT_c, avT_l], axis=-1)
    a_lat = _diffattn(lam, qkv_l, 0, ak_all, 0, avT_all, a_subln, post)
    b_lat = _window_attn(b_sink, qkv_l, 8, 12, 14, qkv_c, with_win=True)
    x_lat = _out_ffn(x_lat, a_lat, b_lat, w_out, g_l, norm_ffn, fsh_l, fsc_l, fg_l, wg, wu, wd, tm=FFN_ROW_TILE)
    if need_ctx:
        a_ctx = _diffattn(lam, qkv_c, 0, qkv_c, 4, avT_c, a_subln, post)
        b_ctx = _window_attn(b_sink, qkv_c, 8, 12, 14, qkv_c, with_win=False)
        x_ctx = _out_ffn(x_ctx, a_ctx, b_ctx, w_out, g_c, norm_ffn, fsh_c, fsc_c, fg_c, wg, wu, wd,
                         tm=x_ctx.shape[1])
    return x_lat, x_ctx


def _pad_block(m):
    return jnp.concatenate([m, jnp.zeros(m.shape[:-1] + (LANES - m.shape[-1],), m.dtype)], axis=-1)


def _cd_layer(x_lat, x_ctx, mods_layer, norm_mix, norm_ffn, wg, wu, wd, w_in, w_out, c_qn, c_kn, c_rpb,
              d_mu, d_w0, d_w2, d_a0, d_a2, d_g2, d_k_k, d_k_a, d_r_k, d_ln_w, d_ln_b):
    b, s_len, d = x_lat.shape
    n_ctx = x_ctx.shape[1]
    (sh_l, sc_l, g_l, fsh_l, fsc_l, fg_l), (sh_c, sc_c, _, _, _, _) = _mod_vectors(mods_layer, b)
    w = D_WIDTH
    lora = lambda m: [_pad_block(m[..., 3 * w:3 * w + 64]), _pad_block(m[..., 3 * w + 64:3 * w + 128]),
                      _pad_block(m[..., 3 * w + 128:])]
    w_d = w_in[:, 3 * w:]
    w_cat = jnp.concatenate([w_in[:, :3 * w], w_d[:, :3 * w]] + lora(w_d), axis=1).astype(BF16)
    mu_p = jnp.concatenate([d_mu[:3 * w]] + lora(d_mu))
    scale = HEAD_DIM ** -0.5
    hg = jnp.stack([_tile2(c_qn) * scale, _tile2(c_kn)] + [jnp.zeros((LANES,), F32)] * 6)
    plan = ([("qk", 0, 0, j) for j in range(4)] + [("qk", 1, 0, 4 + j) for j in range(4)]
            + [("v", 0, 0, 8 + j) for j in range(4)] + [("raw", 0, 1, j) for j in range(15)])
    out_defs = [("rows", 12, BF16), ("rows", 15, F32)]
    no_rope = lambda n: (jnp.ones((n, LANES), F32), jnp.zeros((n, LANES), F32))
    qkv_l, dcols_l = _inproj(x_lat, sh_l, sc_l, norm_mix, w_cat, *no_rope(s_len), hg, plan, out_defs,
                             tm=PROJ_ROW_TILE)
    qkv_c, dcols_c = _inproj(x_ctx, sh_c, sc_c, norm_mix, w_cat, *no_rope(n_ctx), hg, plan, out_defs, tm=n_ctx)

    c_lat = _natten(qkv_l, qkv_c, _natten_bias(c_rpb))

    prep = functools.partial(_rwkv_prep, mu=mu_p, w0=d_w0, w2=d_w2, a0=d_a0, a2=d_a2, g2=d_g2, k_k=d_k_k,
                             tm=PREP_ROW_TILE)
    rkvk_c, ld_c, a_c, _ = prep(dcols_c)
    rkvk_l, ld_l, a_l, g_l_gate = prep(dcols_l)
    s0 = jnp.zeros((b, 2, 4, LANES, LANES), F32)
    _, _, s_ctx = _rwkv_scan(rkvk_c, ld_c, a_c, d_k_a, s0, tile=min(SCAN_TILE, n_ctx))
    y_f, y_r, _ = _rwkv_scan(rkvk_l, ld_l, a_l, d_k_a, s_ctx, tile=SCAN_TILE)
    d_lat = _rwkv_out(y_f, y_r, rkvk_l, a_l, g_l_gate, d_k_a, d_r_k, d_ln_w, d_ln_b, tm=PROJ_ROW_TILE)
    return _out_ffn(x_lat, c_lat, d_lat, w_out, g_l, norm_ffn, fsh_l, fsc_l, fg_l, wg, wu, wd, tm=FFN_ROW_TILE)


def kernel(x, c, ctx, c_ctx, ada_w, ada_b, norm_mix, norm_ffn, ffn_w_gate, ffn_w_up, ffn_w_down, ab_w_in, ab_w_out, a_q_norm, a_k_norm, a_lambda, a_subln, b_q_norm, b_k_norm, b_sink, cd_w_in, cd_w_out, c_q_norm, c_k_norm, c_rpb, d_mu, d_w0, d_w2, d_a0, d_a2, d_g2, d_k_k, d_k_a, d_r_k, d_ln_w, d_ln_b):
    b, s_len, d = x.shape
    n_ctx = ctx.shape[1]
    cond = jnp.concatenate([c, c_ctx[None, :], jnp.zeros((8 - b - 1, d), F32)], axis=0)
    mods = _adaln(cond, ada_w, ada_b)
    rope_lat = _rope_tables(s_len)
    rope_ctx = (jnp.ones((n_ctx, LANES), F32), jnp.zeros((n_ctx, LANES), F32))
    x_lat, x_ctx = x, ctx
    x_lat, x_ctx = _ab_layer(
        x_lat, x_ctx, mods[0], 0, norm_mix[0], norm_ffn[0], ffn_w_gate[0].astype(BF16), ffn_w_up[0].astype(BF16),
        ffn_w_down[0].astype(BF16), ab_w_in[0], ab_w_out[0].astype(BF16), a_q_norm[0], a_k_norm[0], a_lambda[0],
        a_subln[0], b_q_norm[0], b_k_norm[0], b_sink[0], rope_lat, rope_ctx, need_ctx=True)
    return _cd_layer(
        x_lat, x_ctx, mods[1], norm_mix[1], norm_ffn[1], ffn_w_gate[1].astype(BF16), ffn_w_up[1].astype(BF16),
        ffn_w_down[1].astype(BF16), cd_w_in[0], cd_w_out[0].astype(BF16), c_q_norm[0], c_k_norm[0], c_rpb[0],
        d_mu[0], d_w0[0], d_w2[0], d_a0[0], d_a2[0], d_g2[0], d_k_k[0], d_k_a[0], d_r_k[0], d_ln_w[0], d_ln_b[0])
```
